```python
import math
import jax, jax.numpy as jnp
from jax import lax
import numpy as np


D_MODEL = 2048
BATCH = 2
SEQ = 8192
DEPTH = 1

SSM_WIDTH = D_MODEL // 2
SSM_GROUP = 16
SSM_GROUPS = SSM_WIDTH // SSM_GROUP
SSM_STATE = 64
ATTN_WIDTH = D_MODEL - SSM_WIDTH
HEAD_DIM = 128
N_HEADS = ATTN_WIDTH // HEAD_DIM
DILATED_BRANCHES = ((128, 1), (512, 4), (2048, 16))
FFN_HIDDEN = ((8 * D_MODEL + 3 * 256 - 1) // (3 * 256)) * 256
IN_PROJ_WIDTH = SSM_WIDTH + 3 * ATTN_WIDTH
N_MOD = 6
DT_MIN = 1e-3
DT_MAX = 1e-1
NORM_EPS = 1e-6
MASK_VALUE = -1e30

kernel_name = "hybrid_s5_dilated_alibi_block"


def _rms(x, w):
    xf = x.astype(jnp.float32)
    y = xf * lax.rsqrt(jnp.mean(xf * xf, axis=-1, keepdims=True) + NORM_EPS)
    return y * w.astype(jnp.float32)


def _modulate(y, shift, scale):
    return y * (1.0 + scale.astype(jnp.float32)[:, None, :]) + shift.astype(jnp.float32)[:, None, :]


def _complex_linear_combine(e1, e2):
    a1r, a1i, b1r, b1i = e1
    a2r, a2i, b2r, b2i = e2
    ar = a1r * a2r - a1i * a2i
    ai = a1r * a2i + a1i * a2r
    br = a2r * b1r - a2i * b1i + b2r
    bi = a2r * b1i + a2i * b1r + b2i
    return ar, ai, br, bi


def _s5_mixer(u, a_re, a_im, log_dt, b_re, b_im, c_re, c_im, d, w_glu):
    B, S, _ = u.shape
    f32 = jnp.float32
    uf = u.astype(f32).reshape(B, S, SSM_GROUPS, SSM_GROUP)
    a_re = a_re.astype(f32)
    a_im = a_im.astype(f32)
    dt = jnp.exp(log_dt.astype(f32))[:, None]
    mag = jnp.exp(a_re * dt)
    abar_re = mag * jnp.cos(a_im * dt)
    abar_im = mag * jnp.sin(a_im * dt)
    num_re = abar_re - 1.0
    num_im = abar_im
    den = a_re * a_re + a_im * a_im
    z_re = (num_re * a_re + num_im * a_im) / den
    z_im = (num_im * a_re - num_re * a_im) / den
    b_re = b_re.astype(f32)
    b_im = b_im.astype(f32)
    bbar_re = z_re[..., None] * b_re - z_im[..., None] * b_im
    bbar_im = z_re[..., None] * b_im + z_im[..., None] * b_re
    bu_re = jnp.einsum('bsgc,gnc->sbgn', uf, bbar_re)
    bu_im = jnp.einsum('bsgc,gnc->sbgn', uf, bbar_im)
    a_seq_re = jnp.broadcast_to(abar_re[None, None], (S, 1, SSM_GROUPS, SSM_STATE))
    a_seq_im = jnp.broadcast_to(abar_im[None, None], (S, 1, SSM_GROUPS, SSM_STATE))
    _, _, h_re, h_im = lax.associative_scan(
        _complex_linear_combine, (a_seq_re, a_seq_im, bu_re, bu_im), axis=0)
    y = (jnp.einsum('sbgn,gcn->bsgc', h_re, c_re.astype(f32))
         - jnp.einsum('sbgn,gcn->bsgc', h_im, c_im.astype(f32))
         + d.astype(f32) * uf)
    y = jax.nn.gelu(y.reshape(B, S, SSM_WIDTH))
    return y * jax.nn.sigmoid(y @ w_glu.astype(f32))


def _dilated_branch(q, k, v, slopes, window, dilation):
    B, S, H, E = q.shape
    blk = window // dilation
    span = blk * dilation
    s_pad = ((S + span - 1) // span) * span
    pad = s_pad - S
    L = s_pad // dilation
    nb = L // blk

    def strided_blocks(t):
        t = jnp.pad(t, ((0, 0), (0, pad), (0, 0), (0, 0)))
        t = t.reshape(B, L, dilation, H, E).transpose(0, 2, 3, 1, 4)
        return t.reshape(B, dilation, H, nb, blk, E)

    qb, kb, vb = strided_blocks(q), strided_blocks(k), strided_blocks(v)

    def with_prev(t):
        prev = jnp.concatenate([jnp.zeros_like(t[:, :, :, :1]), t[:, :, :, :-1]], axis=3)
        return jnp.concatenate([prev, t], axis=4)

    kc, vc = with_prev(kb), with_prev(vb)
    scores = jnp.einsum('bdhnqe,bdhnke->bdhnqk', qb, kc)
    qi = jnp.arange(blk)[:, None]
    ki = jnp.arange(2 * blk)[None, :]
    dist = qi + blk - ki
    band = (dist >= 0) & (dist <= blk)
    first = (jnp.arange(nb) == 0)[:, None, None]
    valid = band[None] & ~(first & (ki < blk)[None])
    bias = -slopes[:, None, None, None] * (dilation * dist).astype(jnp.float32)[None, None]
    scores = jnp.where(valid, scores + bias, MASK_VALUE)
    m = jnp.max(scores, axis=-1)
    p = jnp.exp(scores - m[..., None])
    l = jnp.sum(p, axis=-1)
    o = jnp.einsum('bdhnqk,bdhnke->bdhnqe', p, vc)

    m = m.reshape(B, dilation, H, L).transpose(0, 3, 1, 2).reshape(B, s_pad, H)[:, :S]
    l = l.reshape(B, dilation, H, L).transpose(0, 3, 1, 2).reshape(B, s_pad, H)[:, :S]
    o = o.reshape(B, dilation, H, L, E).transpose(0, 3, 1, 2, 4).reshape(B, s_pad, H, E)[:, :S]
    return m, l, o


def _dilated_attention(q, k, v, q_norm_w, k_norm_w):
    B, S, _ = q.shape
    q = _rms(q.reshape(B, S, N_HEADS, HEAD_DIM), q_norm_w) * (HEAD_DIM ** -0.5)
    k = _rms(k.reshape(B, S, N_HEADS, HEAD_DIM), k_norm_w)
    v = v.astype(jnp.float32).reshape(B, S, N_HEADS, HEAD_DIM)
    slopes = jnp.exp2(-8.0 * jnp.arange(1, N_HEADS + 1, dtype=jnp.float32) / N_HEADS)
    outs = [_dilated_branch(q, k, v, slopes, w, dl) for (w, dl) in DILATED_BRANCHES]
    ms = jnp.stack([o[0] for o in outs])
    ls = jnp.stack([o[1] for o in outs])
    os_ = jnp.stack([o[2] for o in outs])
    w = jnp.exp(ms - jnp.max(ms, axis=0, keepdims=True))
    num = jnp.sum(w[..., None] * os_, axis=0)
    den = jnp.sum(w * ls, axis=0)
    return (num / den[..., None]).reshape(B, S, ATTN_WIDTH)


def setup_inputs(seed: int = 0) -> dict:
    key = jax.random.key(seed)
    ks = jax.random.split(key, 24)
    f32 = jnp.float32
    nrm = lambda k, shape, s: jax.random.normal(k, shape, f32) * s
    D, G, N, C = D_MODEL, SSM_GROUPS, SSM_STATE, SSM_GROUP
    x = jax.random.normal(ks[0], (BATCH, SEQ, D), f32)
    c = jax.random.normal(ks[1], (BATCH, D), f32)
    w_ada = nrm(ks[2], (DEPTH, D, N_MOD * D), 0.5 * D ** -0.5)
    b_ada = nrm(ks[3], (DEPTH, N_MOD * D), 0.05)
    norm1_w = 1.0 + nrm(ks[4], (DEPTH, D), 0.02)
    w_in = nrm(ks[5], (DEPTH, D, IN_PROJ_WIDTH), D ** -0.5)
    ssm_a_re = -0.5 + nrm(ks[6], (DEPTH, G, N), 0.01)
    ssm_a_im = jnp.pi * jnp.arange(N, dtype=f32)[None, None, :] + nrm(ks[7], (DEPTH, G, N), 0.01)
    ssm_log_dt = jax.random.uniform(ks[8], (DEPTH, G), f32, math.log(DT_MIN), math.log(DT_MAX))
    ssm_b_re = nrm(ks[9], (DEPTH, G, N, C), (2 * C) ** -0.5)
    ssm_b_im = nrm(ks[10], (DEPTH, G, N, C), (2 * C) ** -0.5)
    ssm_c_re = nrm(ks[11], (DEPTH, G, C, N), (2 * N) ** -0.5)
    ssm_c_im = nrm(ks[12], (DEPTH, G, C, N), (2 * N) ** -0.5)
    ssm_d = nrm(ks[13], (DEPTH, G, C), 1.0)
    ssm_w_glu = nrm(ks[14], (DEPTH, SSM_WIDTH, SSM_WIDTH), SSM_WIDTH ** -0.5)
    q_norm_w = 1.0 + nrm(ks[15], (DEPTH, HEAD_DIM), 0.02)
    k_norm_w = 1.0 + nrm(ks[16], (DEPTH, HEAD_DIM), 0.02)
    w_out = nrm(ks[17], (DEPTH, D, D), D ** -0.5)
    norm2_w = 1.0 + nrm(ks[18], (DEPTH, D), 0.02)
    w_ffn_gate = nrm(ks[19], (DEPTH, D, FFN_HIDDEN), D ** -0.5)
    w_ffn_up = nrm(ks[20], (DEPTH, D, FFN_HIDDEN), D ** -0.5)
    w_ffn_down = nrm(ks[21], (DEPTH, FFN_HIDDEN, D), FFN_HIDDEN ** -0.5)
    return {"x": x, "c": c, "w_ada": w_ada, "b_ada": b_ada, "norm1_w": norm1_w,
            "w_in": w_in, "ssm_a_re": ssm_a_re, "ssm_a_im": ssm_a_im,
            "ssm_log_dt": ssm_log_dt, "ssm_b_re": ssm_b_re, "ssm_b_im": ssm_b_im,
            "ssm_c_re": ssm_c_re, "ssm_c_im": ssm_c_im, "ssm_d": ssm_d,
            "ssm_w_glu": ssm_w_glu, "q_norm_w": q_norm_w, "k_norm_w": k_norm_w,
            "w_out": w_out, "norm2_w": norm2_w, "w_ffn_gate": w_ffn_gate,
            "w_ffn_up": w_ffn_up, "w_ffn_down": w_ffn_down}


def reference(x, c, w_ada, b_ada, norm1_w, w_in, ssm_a_re, ssm_a_im, ssm_log_dt,
              ssm_b_re, ssm_b_im, ssm_c_re, ssm_c_im, ssm_d, ssm_w_glu,
              q_norm_w, k_norm_w, w_out, norm2_w, w_ffn_gate, w_ffn_up, w_ffn_down):
    cond = jax.nn.silu(c)
    splits = [SSM_WIDTH, SSM_WIDTH + ATTN_WIDTH, SSM_WIDTH + 2 * ATTN_WIDTH]
    for i in range(DEPTH):
        mod = cond @ w_ada[i] + b_ada[i]
        shift1, scale1, gate1, shift2, scale2, gate2 = jnp.split(mod, N_MOD, axis=-1)

        h = _modulate(_rms(x, norm1_w[i]), shift1, scale1).astype(x.dtype)
        proj = h @ w_in[i]
        u, q, k, v = jnp.split(proj, splits, axis=-1)
        y_ssm = _s5_mixer(u, ssm_a_re[i], ssm_a_im[i], ssm_log_dt[i], ssm_b_re[i], ssm_b_im[i],
                          ssm_c_re[i], ssm_c_im[i], ssm_d[i], ssm_w_glu[i])
        y_att = _dilated_attention(q, k, v, q_norm_w[i], k_norm_w[i])
        mixed = jnp.concatenate([y_ssm, y_att], axis=-1).astype(x.dtype) @ w_out[i]
        x = x + gate1[:, None, :] * mixed

        h2 = _modulate(_rms(x, norm2_w[i]), shift2, scale2).astype(x.dtype)
        ffn = (jax.nn.silu(h2 @ w_ffn_gate[i]) * (h2 @ w_ffn_up[i])) @ w_ffn_down[i]
        x = x + gate2[:, None, :] * ffn
    return x
```

```python
import functools

import numpy as np
import jax
import jax.numpy as jnp
from jax import lax
from jax.experimental import pallas as pl
from jax.experimental.pallas import tpu as pltpu

F32 = jnp.float32
BF16 = jnp.bfloat16

D_MODEL = 2048
BATCH = 2
SEQ = 8192
TOKENS = BATCH * SEQ
SSM_WIDTH = 1024
SSM_GROUP = 16
SSM_GROUPS = 64
SSM_STATE = 64
ATTN_WIDTH = 1024
HEAD_DIM = 128
N_HEADS = 8
BRANCH_DILATIONS = (16, 4, 1)
BRANCH_BLOCK = 128
FFN_HIDDEN = 5632
N_MOD = 6
NORM_EPS = 1e-6
MASK_VALUE = -1e30

N_SEG = 8
SEG_LEN = TOKENS // N_SEG
SEG_PER_BATCH = SEQ // SEG_LEN
SSM_COLSETS = 4
COLSET_CH = SSM_WIDTH // SSM_COLSETS
COLSET_ST = 16 * SSM_STATE
SSM_CHUNK = 64
SSM_NCHUNK = SEG_LEN // SSM_CHUNK

ATT_SUPER = 2048
ATT_ROWS = ATT_SUPER // 16
ATT_NSUPER = SEQ // ATT_SUPER

ADA_TN = 1024
INPROJ_TM = 512
MIX_TM = 512
FFN_TM = 512
FFN_TH = 512

VMEM_LIMIT = 56 * 1024 * 1024


def _sigmoid(x):
    return 1.0 / (1.0 + jnp.exp(-x))


def _ada_kernel(ct_ref, w_ref, b_ref, o_ref, cond_scr):
    c = ct_ref[...]
    cond_scr[...] = c * _sigmoid(c)

    def body(kc, acc):
        row = pl.multiple_of(kc * 8, 8)
        wk = w_ref[pl.ds(row, 8), :]
        ck = cond_scr[pl.ds(row, 8), :]
        return tuple(acc[b] + ck[:, b:b + 1] * wk for b in range(BATCH))

    zero = jnp.zeros((8, ADA_TN), F32)
    acc = lax.fori_loop(0, D_MODEL // 8, body, (zero,) * BATCH, unroll=8)
    rows = [jnp.sum(a, axis=0, keepdims=True) for a in acc]
    o_ref[...] = jnp.concatenate(rows, axis=0) + b_ref[...]


def _ada(c, w_ada, b_ada):
    n = N_MOD * D_MODEL
    return pl.pallas_call(
        _ada_kernel,
        grid=(n // ADA_TN,),
        in_specs=[
            pl.BlockSpec((D_MODEL, BATCH), lambda j: (0, 0)),
            pl.BlockSpec((D_MODEL, ADA_TN), lambda j: (0, j)),
            pl.BlockSpec((1, ADA_TN), lambda j: (0, j)),
        ],
        out_specs=pl.BlockSpec((BATCH, ADA_TN), lambda j: (0, j)),
        out_shape=jax.ShapeDtypeStruct((BATCH, n), F32),
        scratch_shapes=[pltpu.VMEM((D_MODEL, BATCH), F32)],
        compiler_params=pltpu.CompilerParams(
            dimension_semantics=("arbitrary",), vmem_limit_bytes=VMEM_LIMIT),
        name="ada",
    )(c.T, w_ada, b_ada.reshape(1, n))


def _head_rms(blk, w):
    ms = jnp.mean(blk * blk, axis=-1, keepdims=True)
    return blk * lax.rsqrt(ms + NORM_EPS) * w


def _inproj_kernel(x_ref, sh_ref, sc_ref, nw_ref, w_ref, qw_ref, kw_ref,
                   u_ref, q_ref, k_ref, v_ref, h_scr):
    j = pl.program_id(1)

    @pl.when(j == 0)
    def _():
        x = x_ref[...]
        ms = jnp.mean(x * x, axis=-1, keepdims=True)
        y = x * lax.rsqrt(ms + NORM_EPS) * nw_ref[...]
        h_scr[...] = (y * (1.0 + sc_ref[0]) + sh_ref[0]).astype(BF16)

    res = jnp.dot(h_scr[...], w_ref[...], preferred_element_type=F32)

    @pl.when(j == 0)
    def _():
        u_ref[...] = res

    @pl.when(j == 1)
    def _():
        for h in range(N_HEADS):
            blk = res[:, h * HEAD_DIM:(h + 1) * HEAD_DIM]
            q_ref[0, h] = (_head_rms(blk, qw_ref[...]) * (HEAD_DIM ** -0.5)).astype(BF16)

    @pl.when(j == 2)
    def _():
        for h in range(N_HEADS):
            blk = res[:, h * HEAD_DIM:(h + 1) * HEAD_DIM]
            k_ref[0, h] = _head_rms(blk, kw_ref[...]).astype(BF16)

    @pl.when(j == 3)
    def _():
        for h in range(N_HEADS):
            v_ref[0, h] = res[:, h * HEAD_DIM:(h + 1) * HEAD_DIM].astype(BF16)


def _inproj(x2, shift1, scale1, norm1_w, w_in_bf, q_norm_w, k_norm_w):
    tm = INPROJ_TM
    per_seg = SEG_LEN // tm
    per_batch = SEQ // tm
    qkv_shape = jax.ShapeDtypeStruct((BATCH, N_HEADS, SEQ, HEAD_DIM), BF16)
    qkv_spec = pl.BlockSpec((1, N_HEADS, tm, HEAD_DIM),
                            lambda i, j: (i // per_batch, 0, i % per_batch, 0))
    mod_spec = pl.BlockSpec((1, 1, D_MODEL), lambda i, j: (i // per_batch, 0, 0))
    return pl.pallas_call(
        _inproj_kernel,
        grid=(TOKENS // tm, 4),
        in_specs=[
            pl.BlockSpec((tm, D_MODEL), lambda i, j: (i, 0)),
            mod_spec, mod_spec,
            pl.BlockSpec((1, D_MODEL), lambda i, j: (0, 0)),
            pl.BlockSpec((D_MODEL, 1024), lambda i, j: (0, j)),
            pl.BlockSpec((1, HEAD_DIM), lambda i, j: (0, 0)),
            pl.BlockSpec((1, HEAD_DIM), lambda i, j: (0, 0)),
        ],
        out_specs=[
            pl.BlockSpec((tm, SSM_WIDTH), lambda i, j: (i % per_seg, i // per_seg)),
            qkv_spec, qkv_spec, qkv_spec,
        ],
        out_shape=[
            jax.ShapeDtypeStruct((SEG_LEN, N_SEG * SSM_WIDTH), F32),
            qkv_shape, qkv_shape, qkv_shape,
        ],
        scratch_shapes=[pltpu.VMEM((tm, D_MODEL), BF16)],
        compiler_params=pltpu.CompilerParams(
            dimension_semantics=("parallel", "arbitrary"), vmem_limit_bytes=VMEM_LIMIT),
        name="inproj",
    )(x2, shift1, scale1, norm1_w, w_in_bf, q_norm_w, k_norm_w)


def _ssm_prep_kernel(are_ref, aim_ref, ldt_ref, bre_ref, bim_ref, cim_ref,
                     abr_ref, abi_ref, bbr_ref, bbi_ref, ncim_ref):
    a_re = are_ref[...]
    a_im = aim_ref[...]
    dt = jnp.exp(ldt_ref[...])
    mag = jnp.exp(a_re * dt)
    abar_re = mag * jnp.cos(a_im * dt)
    abar_im = mag * jnp.sin(a_im * dt)
    num_re = abar_re - 1.0
    num_im = abar_im
    den = a_re * a_re + a_im * a_im
    z_re = (num_re * a_re + num_im * a_im) / den
    z_im = (num_im * a_re - num_re * a_im) / den
    b_re = bre_ref[...]
    b_im = bim_ref[...]
    abr_ref[...] = abar_re
    abi_ref[...] = abar_im
    bbr_ref[...] = z_re * b_re - z_im * b_im
    bbi_ref[...] = z_re * b_im + z_im * b_re
    ncim_ref[...] = -cim_ref[...]


def _ssm_prep(a_re, a_im, log_dt, b_re, b_im, c_re, c_im):
    G, N, C = SSM_GROUPS, SSM_STATE, SSM_GROUP
    tile_n = lambda a: jnp.tile(a, (1, C))
    cn = lambda b: b.transpose(0, 2, 1).reshape(G, C * N)
    shp = jax.ShapeDtypeStruct((G, C * N), F32)
    abr, abi, bbr, bbi, ncim = pl.pallas_call(
        _ssm_prep_kernel,
        out_shape=[shp] * 5,
        name="ssm_prep",
    )(tile_n(a_re), tile_n(a_im), log_dt.reshape(G, 1), cn(b_re), cn(b_im),
      c_im.reshape(G, C * N))

    eye = jnp.eye(16, dtype=F32)
    a_all = jnp.stack([abr[:, :N], abi[:, :N]])
    a_all = a_all.reshape(2, SSM_COLSETS, 16 * N).transpose(1, 0, 2)
    a_all = a_all.reshape(SSM_COLSETS, 1, 2 * COLSET_ST)
    bb = jnp.stack([bbr, bbi]).reshape(2, SSM_COLSETS, 16, C, N)
    b_exp = jnp.einsum('psgcn,gh->sgcphn', bb, eye)
    b_exp = b_exp.reshape(SSM_COLSETS, COLSET_CH, 2 * COLSET_ST).astype(BF16)
    cc = jnp.stack([c_re.reshape(G, C * N), ncim]).reshape(2, SSM_COLSETS, 16, C, N)
    c_exp = jnp.einsum('psgcn,gh->sphngc', cc, eye)
    c_exp = c_exp.reshape(SSM_COLSETS, 2 * COLSET_ST, COLSET_CH).astype(BF16)
    return a_all, b_exp, c_exp


def _cmul(ar, ai, br, bi):
    return ar * br - ai * bi, ar * bi + ai * br


def _ssm_scan(bu_scr, ar, ai, hr, hi, store):
    def body(t, carry):
        hr, hi = carry
        row = pl.multiple_of(t * N_SEG, N_SEG)
        br = bu_scr[pl.ds(row, N_SEG), :COLSET_ST]
        bi = bu_scr[pl.ds(row, N_SEG), COLSET_ST:]
        nr = ar * hr - ai * hi + br
        ni = ar * hi + ai * hr + bi
        if store:
            bu_scr[pl.ds(row, N_SEG), :COLSET_ST] = nr
            bu_scr[pl.ds(row, N_SEG), COLSET_ST:] = ni
        return nr, ni

    return lax.fori_loop(0, SSM_CHUNK, body, (hr, hi), unroll=4)


def _ssm_a_kernel(u_ref, b_ref, a_ref, hfin_ref, bu_scr, h_scr):
    ch = pl.program_id(1)

    @pl.when(ch == 0)
    def _():
        h_scr[...] = jnp.zeros_like(h_scr)

    bu_scr[...] = jnp.dot(u_ref[...].astype(BF16), b_ref[0], preferred_element_type=F32)
    a = a_ref[0]
    ar = jnp.broadcast_to(a[:, :COLSET_ST], (N_SEG, COLSET_ST))
    ai = jnp.broadcast_to(a[:, COLSET_ST:], (N_SEG, COLSET_ST))
    hr, hi = _ssm_scan(bu_scr, ar, ai, h_scr[:, :COLSET_ST], h_scr[:, COLSET_ST:], False)
    h_scr[:, :COLSET_ST] = hr
    h_scr[:, COLSET_ST:] = hi

    @pl.when(ch == SSM_NCHUNK - 1)
    def _():
        hfin_ref[0] = h_scr[...]


def _ssm_b_kernel(u_ref, b_ref, c_ref, a_ref, d_ref, hfin_ref, y_ref, bu_scr, h_scr):
    ch = pl.program_id(1)
    a = a_ref[0]
    ar = jnp.broadcast_to(a[:, :COLSET_ST], (N_SEG, COLSET_ST))
    ai = jnp.broadcast_to(a[:, COLSET_ST:], (N_SEG, COLSET_ST))

    @pl.when(ch == 0)
    def _():
        f = hfin_ref[0]
        fr, fi = f[:, :COLSET_ST], f[:, COLSET_ST:]
        pr, pi = ar, ai
        for _ in range(SEG_LEN.bit_length() - 1):
            pr, pi = _cmul(pr, pi, pr, pi)
        q = lax.broadcasted_iota(jnp.int32, (N_SEG, COLSET_ST), 0) % SEG_PER_BATCH

        def shifted(v, k):
            return jnp.where(q >= k, pltpu.roll(v, k, 0), 0.0)

        ir, ii = shifted(fr, 1), shifted(fi, 1)
        wr, wi = pr, pi
        for k in range(2, SEG_PER_BATCH):
            tr, ti = _cmul(wr, wi, shifted(fr, k), shifted(fi, k))
            ir, ii = ir + tr, ii + ti
            wr, wi = _cmul(wr, wi, pr, pi)
        h_scr[:, :COLSET_ST] = ir
        h_scr[:, COLSET_ST:] = ii

    u = u_ref[...]
    bu_scr[...] = jnp.dot(u.astype(BF16), b_ref[0], preferred_element_type=F32)
    hr, hi = _ssm_scan(bu_scr, ar, ai, h_scr[:, :COLSET_ST], h_scr[:, COLSET_ST:], True)
    h_scr[:, :COLSET_ST] = hr
    h_scr[:, COLSET_ST:] = hi
    y = jnp.dot(bu_scr[...].astype(BF16), c_ref[0], preferred_element_type=F32)
    y_ref[...] = jax.nn.gelu(y + d_ref[...] * u)


def _ssm(u_il, a_all, b_exp, c_exp, d_flat):
    rows = N_SEG * SSM_CHUNK
    u_spec = pl.BlockSpec((rows, COLSET_CH), lambda s, c: (c, s))
    a_spec = pl.BlockSpec((1, 1, 2 * COLSET_ST), lambda s, c: (s, 0, 0))
    b_spec = pl.BlockSpec((1, COLSET_CH, 2 * COLSET_ST), lambda s, c: (s, 0, 0))
    hfin_spec = pl.BlockSpec((1, N_SEG, 2 * COLSET_ST), lambda s, c: (s, 0, 0))
    scratch = [pltpu.VMEM((rows, 2 * COLSET_ST), F32), pltpu.VMEM((N_SEG, 2 * COLSET_ST), F32)]
    params = pltpu.CompilerParams(
        dimension_semantics=("parallel", "arbitrary"), vmem_limit_bytes=VMEM_LIMIT)
    hfin = pl.pallas_call(
        _ssm_a_kernel,
        grid=(SSM_COLSETS, SSM_NCHUNK),
        in_specs=[u_spec, b_spec, a_spec],
        out_specs=hfin_spec,
        out_shape=jax.ShapeDtypeStruct((SSM_COLSETS, N_SEG, 2 * COLSET_ST), F32),
        scratch_shapes=scratch,
        compiler_params=params,
        name="ssm_a",
    )(u_il, b_exp, a_all)
    return pl.pallas_call(
        _ssm_b_kernel,
        grid=(SSM_COLSETS, SSM_NCHUNK),
        in_specs=[
            u_spec, b_spec,
            pl.BlockSpec((1, 2 * COLSET_ST, COLSET_CH), lambda s, c: (s, 0, 0)),
            a_spec,
            pl.BlockSpec((1, COLSET_CH), lambda s, c: (0, s)),
            hfin_spec,
        ],
        out_specs=u_spec,
        out_shape=jax.ShapeDtypeStruct((TOKENS, SSM_WIDTH), F32),
        scratch_shapes=scratch,
        compiler_params=params,
        name="ssm_b",
    )(u_il, b_exp, c_exp, a_all, d_flat, hfin)


def _branch_distances():
    out = []
    for dil in BRANCH_DILATIONS:
        per = 8 * dil
        idx = np.arange(BRANCH_BLOCK)
        pos = (BRANCH_BLOCK // per) * (idx % per) + idx // per
        dist = np.concatenate([pos[:, None] - pos[None, :] + BRANCH_BLOCK,
                               pos[:, None] - pos[None, :]], axis=1)
        ok = (dist >= 0) & (dist <= BRANCH_BLOCK)
        out.append(np.where(ok, dil * dist, -1).astype(np.float32))
    return np.stack(out)


def _attn_tile(qt, kt, vt, bm):
    s = lax.dot_general(qt, kt, (((1,), (1,)), ((), ())), preferred_element_type=F32)
    s = jnp.where(bm > 0.5 * MASK_VALUE, s + bm, MASK_VALUE)
    m = jnp.max(s, axis=-1, keepdims=True)
    p = jnp.exp(s - m)
    l = jnp.sum(p, axis=-1, keepdims=True)
    o = jnp.dot(p.astype(BF16), vt, preferred_element_type=F32)
    return m, l, o


def _attn_kernel(q_ref, k_ref, v_ref, dd_ref, o_ref,
                 q_scr, k_scr, v_scr, m_scr, l_scr, o_scr, bias_scr):
    h = pl.program_id(1)
    i = pl.program_id(2)
    R = ATT_ROWS
    E = HEAD_DIM

    @pl.when(i == 0)
    def _():
        k_scr[0:R, :] = jnp.zeros((R, 16 * E), F32)
        v_scr[0:R, :] = jnp.zeros((R, 16 * E), F32)

    q_scr[...] = q_ref[0, 0].astype(F32)
    k_scr[R:2 * R, :] = k_ref[0, 0].astype(F32)
    v_scr[R:2 * R, :] = v_ref[0, 0].astype(F32)

    hv = jnp.full((BRANCH_BLOCK, 2 * BRANCH_BLOCK), h, jnp.int32).astype(F32)
    slope = jnp.exp2(-8.0 * (hv + 1.0) / N_HEADS)
    is_prev = lax.broadcasted_iota(jnp.int32, (BRANCH_BLOCK, 2 * BRANCH_BLOCK), 1) < BRANCH_BLOCK
    for br in range(3):
        dd = dd_ref[br]
        bm = jnp.where(dd >= 0.0, (-slope) * dd, MASK_VALUE)
        bias_scr[2 * br] = bm
        bias_scr[2 * br + 1] = jnp.where(is_prev, MASK_VALUE, bm)

    def lanes(r):
        return slice(r * E, (r + 1) * E)

    def gather(scr, row0, nrows, rs):
        return jnp.concatenate([scr[pl.ds(row0, nrows), lanes(r)] for r in rs], axis=0)

    def kv_tile(scr, row0, nrows, rs):
        prev = gather(scr, row0 + R - nrows, nrows, rs)
        cur = gather(scr, row0 + R, nrows, rs)
        return jnp.concatenate([prev, cur], axis=0).astype(BF16)

    def merge(row0, nrows, rs, m_t, l_t, o_t, first, last):
        for n, r in enumerate(rs):
            sl = slice(n * nrows, (n + 1) * nrows)
            idx = (pl.ds(row0, nrows), lanes(r))
            m_c = jnp.broadcast_to(m_t[sl], (nrows, E))
            l_c = jnp.broadcast_to(l_t[sl], (nrows, E))
            o_c = o_t[sl]
            if first:
                m_scr[idx] = m_c
                l_scr[idx] = l_c
                o_scr[idx] = o_c
                continue
            m_old = m_scr[idx]
            m_new = jnp.maximum(m_old, m_c)
            alpha = jnp.exp(m_old - m_new)
            beta = jnp.exp(m_c - m_new)
            l_new = alpha * l_scr[idx] + beta * l_c
            o_new = alpha * o_scr[idx] + beta * o_c
            if last:
                o_scr[idx] = o_new / l_new
            else:
                m_scr[idx] = m_new
                l_scr[idx] = l_new
                o_scr[idx] = o_new

    def run_tile(br, row0, nrows, rs, no_prev):
        qt = gather(q_scr, row0, nrows, rs).astype(BF16)
        kt = kv_tile(k_scr, row0, nrows, rs)
        vt = kv_tile(v_scr, row0, nrows, rs)
        bm = bias_scr[2 * br + no_prev.astype(jnp.int32)]
        m_t, l_t, o_t = _attn_tile(qt, kt, vt, bm)
        merge(row0, nrows, rs, m_t, l_t, o_t, br == 0, br == 2)

    for r in range(16):
        run_tile(0, 0, R, [r], i == 0)

    def d4_body(b4, carry):
        row0 = pl.multiple_of(b4 * 32, 32)
        for r4 in range(4):
            run_tile(1, row0, 32, [r4 + 4 * n for n in range(4)],
                     jnp.logical_and(i == 0, b4 == 0))
        return carry

    lax.fori_loop(0, 4, d4_body, 0)

    def d1_body(b1, carry):
        row0 = pl.multiple_of(b1 * 8, 8)
        run_tile(2, row0, 8, list(range(16)), jnp.logical_and(i == 0, b1 == 0))
        return carry

    lax.fori_loop(0, 16, d1_body, 0)

    o_ref[0, 0] = o_scr[...].astype(BF16)
    k_scr[0:R, :] = k_scr[R:2 * R, :]
    v_scr[0:R, :] = v_scr[R:2 * R, :]


def _attention(q, k, v):
    R = ATT_ROWS
    W = 16 * HEAD_DIM
    view = lambda t: t.reshape(BATCH, N_HEADS, SEQ // 16, W)
    blk = pl.BlockSpec((1, 1, R, W), lambda b, h, i: (b, h, i, 0))
    dd = jnp.asarray(_branch_distances())
    out = pl.pallas_call(
        _attn_kernel,
        grid=(BATCH, N_HEADS, ATT_NSUPER),
        in_specs=[blk, blk, blk,
                  pl.BlockSpec((3, BRANCH_BLOCK, 2 * BRANCH_BLOCK), lambda b, h, i: (0, 0, 0))],
        out_specs=blk,
        out_shape=jax.ShapeDtypeStruct((BATCH, N_HEADS, SEQ // 16, W), BF16),
        scratch_shapes=[
            pltpu.VMEM((R, W), F32),
            pltpu.VMEM((2 * R, W), F32),
            pltpu.VMEM((2 * R, W), F32),
            pltpu.VMEM((R, W), F32),
            pltpu.VMEM((R, W), F32),
            pltpu.VMEM((R, W), F32),
            pltpu.VMEM((6, BRANCH_BLOCK, 2 * BRANCH_BLOCK), F32),
        ],
        compiler_params=pltpu.CompilerParams(
            dimension_semantics=("parallel", "parallel", "arbitrary"),
            vmem_limit_bytes=VMEM_LIMIT),
        name="attn",
    )(view(q), view(k), view(v), dd)
    return out.reshape(BATCH, N_HEADS, SEQ, HEAD_DIM)


def _mix_kernel(x_ref, y_ref, att_ref, g_ref, wglu_ref, wout_ref, o_ref, cat_scr):
    y = y_ref[...]
    gl = jnp.dot(y.astype(BF16), wglu_ref[...], preferred_element_type=F32)
    cat_scr[:, :SSM_WIDTH] = (y * _sigmoid(gl)).astype(BF16)
    for h in range(N_HEADS):
        cat_scr[:, SSM_WIDTH + h * HEAD_DIM:SSM_WIDTH + (h + 1) * HEAD_DIM] = att_ref[0, h]
    mixed = jnp.dot(cat_scr[...], wout_ref[...], preferred_element_type=F32)
    o_ref[...] = x_ref[...] + g_ref[0] * mixed


def _mix(x2, y_il, att, gate1, w_glu_bf, w_out_bf):
    tm = MIX_TM
    per_seg = SEG_LEN // tm
    per_batch = SEQ // tm
    return pl.pallas_call(
        _mix_kernel,
        grid=(TOKENS // tm,),
        in_specs=[
            pl.BlockSpec((tm, D_MODEL), lambda i: (i, 0)),
            pl.BlockSpec((tm, SSM_WIDTH), lambda i: (i % per_seg, i // per_seg)),
            pl.BlockSpec((1, N_HEADS, tm, HEAD_DIM),
                         lambda i: (i // per_batch, 0, i % per_batch, 0)),
            pl.BlockSpec((1, 1, D_MODEL), lambda i: (i // per_batch, 0, 0)),
            pl.BlockSpec((SSM_WIDTH, SSM_WIDTH), lambda i: (0, 0)),
            pl.BlockSpec((D_MODEL, D_MODEL), lambda i: (0, 0)),
        ],
        out_specs=pl.BlockSpec((tm, D_MODEL), lambda i: (i, 0)),
        out_shape=jax.ShapeDtypeStruct((TOKENS, D_MODEL), F32),
        scratch_shapes=[pltpu.VMEM((tm, D_MODEL), BF16)],
        compiler_params=pltpu.CompilerParams(
            dimension_semantics=("parallel",), vmem_limit_bytes=VMEM_LIMIT),
        name="mix",
    )(x2, y_il, att, gate1, w_glu_bf, w_out_bf)


def _ffn_kernel(x_ref, sh_ref, sc_ref, g_ref, nw_ref, wg_ref, wu_ref, wd_ref, o_ref, h_scr):
    j = pl.program_id(1)

    @pl.when(j == 0)
    def _():
        x = x_ref[...]
        ms = jnp.mean(x * x, axis=-1, keepdims=True)
        y = x * lax.rsqrt(ms + NORM_EPS) * nw_ref[...]
        h_scr[...] = (y * (1.0 + sc_ref[0]) + sh_ref[0]).astype(BF16)

    h = h_scr[...]
    g = jnp.dot(h, wg_ref[...], preferred_element_type=F32)
    u = jnp.dot(h, wu_ref[...], preferred_element_type=F32)
    a = (g * _sigmoid(g) * u).astype(BF16)
    part = jnp.dot(a, wd_ref[...], preferred_element_type=F32)

    @pl.when(j == 0)
    def _():
        o_ref[...] = part

    @pl.when(j > 0)
    def _():
        o_ref[...] += part

    @pl.when(j == FFN_HIDDEN // FFN_TH - 1)
    def _():
        o_ref[...] = x_ref[...] + g_ref[0] * o_ref[...]


def _ffn(x1, shift2, scale2, gate2, norm2_w, wg_bf, wu_bf, wd_bf):
    tm, th = FFN_TM, FFN_TH
    per_batch = SEQ // tm
    mod_spec = pl.BlockSpec((1, 1, D_MODEL), lambda i, j: (i // per_batch, 0, 0))
    return pl.pallas_call(
        _ffn_kernel,
        grid=(TOKENS // tm, FFN_HIDDEN // th),
        in_specs=[
            pl.BlockSpec((tm, D_MODEL), lambda i, j: (i, 0)),
            mod_spec, mod_spec, mod_spec,
            pl.BlockSpec((1, D_MODEL), lambda i, j: (0, 0)),
            pl.BlockSpec((D_MODEL, th), lambda i, j: (0, j)),
            pl.BlockSpec((D_MODEL, th), lambda i, j: (0, j)),
            pl.BlockSpec((th, D_MODEL), lambda i, j: (j, 0)),
        ],
        out_specs=pl.BlockSpec((tm, D_MODEL), lambda i, j: (i, 0)),
        out_shape=jax.ShapeDtypeStruct((TOKENS, D_MODEL), F32),
        scratch_shapes=[pltpu.VMEM((tm, D_MODEL), BF16)],
        compiler_params=pltpu.CompilerParams(
            dimension_semantics=("parallel", "arbitrary"), vmem_limit_bytes=VMEM_LIMIT),
        name="ffn",
    )(x1, shift2, scale2, gate2, norm2_w, wg_bf, wu_bf, wd_bf)


def kernel(x, c, w_ada, b_ada, norm1_w, w_in, ssm_a_re, ssm_a_im, ssm_log_dt, ssm_b_re, ssm_b_im,
           ssm_c_re, ssm_c_im, ssm_d, ssm_w_glu, q_norm_w, k_norm_w, w_out, norm2_w,
           w_ffn_gate, w_ffn_up, w_ffn_down):
    x2 = x.reshape(TOKENS, D_MODEL)
    for i in range(w_ada.shape[0]):
        mod = _ada(c, w_ada[i], b_ada[i])
        shift1, scale1, gate1, shift2, scale2, gate2 = [
            m.reshape(BATCH, 1, D_MODEL) for m in jnp.split(mod, N_MOD, axis=-1)]

        u_il, q, k, v = _inproj(x2, shift1, scale1, norm1_w[i].reshape(1, D_MODEL),
                                w_in[i].astype(BF16),
                                q_norm_w[i].reshape(1, HEAD_DIM), k_norm_w[i].reshape(1, HEAD_DIM))

        a_all, b_exp, c_exp = _ssm_prep(ssm_a_re[i], ssm_a_im[i], ssm_log_dt[i], ssm_b_re[i],
                                        ssm_b_im[i], ssm_c_re[i], ssm_c_im[i])
        y_il = _ssm(u_il.reshape(TOKENS, SSM_WIDTH), a_all, b_exp, c_exp,
                    ssm_d[i].reshape(1, SSM_WIDTH))
        att = _attention(q, k, v)

        x2 = _mix(x2, y_il.reshape(SEG_LEN, N_SEG * SSM_WIDTH), att, gate1,
                  ssm_w_glu[i].astype(BF16), w_out[i].astype(BF16))
        x2 = _ffn(x2, shift2, scale2, gate2, norm2_w[i].reshape(1, D_MODEL),
                  w_ffn_gate[i].astype(BF16), w_ffn_up[i].astype(BF16),
                  w_ffn_down[i].astype(BF16))
    return x2.reshape(BATCH, SEQ, D_MODEL)
```

```python
import numpy as np
import jax
import jax.numpy as jnp
from jax import lax
from jax.experimental import pallas as pl
from jax.experimental.pallas import tpu as pltpu

F32 = jnp.float32
BF16 = jnp.bfloat16

D_MODEL = 2048
BATCH = 2
SEQ = 8192
TOKENS = BATCH * SEQ
SSM_WIDTH = 1024
SSM_GROUP = 16
SSM_GROUPS = 64
SSM_STATE = 64
ATTN_WIDTH = 1024
HEAD_DIM = 128
N_HEADS = 8
BRANCH_DILATIONS = (16, 4, 1)
BRANCH_BLOCK = 128
FFN_HIDDEN = 5632
N_MOD = 6
NORM_EPS = 1e-6
MASK_VALUE = -1e30
LANES = 128

N_SEG = 8
SEG_LEN = TOKENS // N_SEG
SEG_PER_BATCH = SEQ // SEG_LEN
ROW_TAU = 64
ROWS = N_SEG * ROW_TAU
N_ROWBLK = SEG_LEN // ROW_TAU
SSM_LB = SSM_WIDTH // LANES
SSM_COLSETS = 4
COLSET_CH = SSM_WIDTH // SSM_COLSETS
COLSET_ST = 16 * SSM_STATE

ATT_Q = SEG_LEN // 4

ADA_TN = 1024
FFN_TM = 512
FFN_TH = 512

VMEM_LIMIT = 56 * 1024 * 1024


def _sigmoid(x):
    return 1.0 / (1.0 + jnp.exp(-x))


def _modulated_norm(x, nw, sh_ref, sc_ref, out_scr, row_batches):
    ms = jnp.mean(x * x, axis=-1, keepdims=True)
    y = x * lax.rsqrt(ms + NORM_EPS) * nw
    for rows, b in row_batches:
        out_scr[rows, :] = (y[rows] * (1.0 + sc_ref[b]) + sh_ref[b]).astype(BF16)


_HALVES = ((slice(0, ROWS // 2), 0), (slice(ROWS // 2, ROWS), 1))


def _ada_kernel(ct_ref, w_ref, b_ref, o_ref, cond_scr):
    c = ct_ref[...]
    cond_scr[...] = c * _sigmoid(c)

    def body(kc, acc):
        row = pl.multiple_of(kc * 8, 8)
        wk = w_ref[pl.ds(row, 8), :]
        ck = cond_scr[pl.ds(row, 8), :]
        return tuple(acc[b] + ck[:, b:b + 1] * wk for b in range(BATCH))

    zero = jnp.zeros((8, ADA_TN), F32)
    acc = lax.fori_loop(0, D_MODEL // 8, body, (zero,) * BATCH, unroll=8)
    rows = [jnp.sum(a, axis=0, keepdims=True) for a in acc]
    o_ref[...] = jnp.concatenate(rows, axis=0) + b_ref[...]


def _ada(c, w_ada, b_ada):
    n = N_MOD * D_MODEL
    return pl.pallas_call(
        _ada_kernel,
        grid=(n // ADA_TN,),
        in_specs=[
            pl.BlockSpec((D_MODEL, BATCH), lambda j: (0, 0)),
            pl.BlockSpec((D_MODEL, ADA_TN), lambda j: (0, j)),
            pl.BlockSpec((1, ADA_TN), lambda j: (0, j)),
        ],
        out_specs=pl.BlockSpec((BATCH, ADA_TN), lambda j: (0, j)),
        out_shape=jax.ShapeDtypeStruct((BATCH, n), F32),
        scratch_shapes=[pltpu.VMEM((D_MODEL, BATCH), F32)],
        compiler_params=pltpu.CompilerParams(
            dimension_semantics=("arbitrary",), vmem_limit_bytes=VMEM_LIMIT),
        name="ada",
    )(c.T, w_ada, b_ada.reshape(1, n))


def _head_rms(blk, w):
    ms = jnp.mean(blk * blk, axis=-1, keepdims=True)
    return blk * lax.rsqrt(ms + NORM_EPS) * w


def _seg_rows(seg):
    return slice(seg * ROW_TAU, (seg + 1) * ROW_TAU)


def _inproj_kernel(x_ref, sh_ref, sc_ref, nw_ref, w_ref, qw_ref, kw_ref,
                   u_ref, q_ref, k_ref, v_ref, h_scr):
    j = pl.program_id(1)

    @pl.when(j == 0)
    def _():
        x = x_ref[...].reshape(ROWS, D_MODEL)
        _modulated_norm(x, nw_ref[...], sh_ref, sc_ref, h_scr, _HALVES)

    res = jnp.dot(h_scr[...], w_ref[...], preferred_element_type=F32)

    def per_head(out_ref, fn):
        for h in range(N_HEADS):
            blk = fn(res[:, h * HEAD_DIM:(h + 1) * HEAD_DIM]).astype(BF16)
            for seg in range(N_SEG):
                out_ref[seg, h] = blk[_seg_rows(seg)]

    @pl.when(j == 0)
    def _():
        for lb in range(SSM_LB):
            for seg in range(N_SEG):
                u_ref[lb, pl.ds(seg, ROW_TAU, stride=N_SEG), :] = (
                    res[_seg_rows(seg), lb * LANES:(lb + 1) * LANES])

    @pl.when(j == 1)
    def _():
        per_head(q_ref, lambda blk: _head_rms(blk, qw_ref[...]) * (HEAD_DIM ** -0.5))

    @pl.when(j == 2)
    def _():
        per_head(k_ref, lambda blk: _head_rms(blk, kw_ref[...]))

    @pl.when(j == 3)
    def _():
        per_head(v_ref, lambda blk: blk)


def _inproj(x3, shift1, scale1, norm1_w, w_in_bf, q_norm_w, k_norm_w):
    qkv_shape = jax.ShapeDtypeStruct((N_SEG, N_HEADS, SEG_LEN, HEAD_DIM), BF16)
    qkv_spec = pl.BlockSpec((N_SEG, N_HEADS, ROW_TAU, HEAD_DIM), lambda i, j: (0, 0, i, 0))
    mod_spec = pl.BlockSpec((BATCH, 1, D_MODEL), lambda i, j: (0, 0, 0))
    return pl.pallas_call(
        _inproj_kernel,
        grid=(N_ROWBLK, 4),
        in_specs=[
            pl.BlockSpec((N_SEG, ROW_TAU, D_MODEL), lambda i, j: (0, i, 0)),
            mod_spec, mod_spec,
            pl.BlockSpec((1, D_MODEL), lambda i, j: (0, 0)),
            pl.BlockSpec((D_MODEL, 1024), lambda i, j: (0, j)),
            pl.BlockSpec((1, HEAD_DIM), lambda i, j: (0, 0)),
            pl.BlockSpec((1, HEAD_DIM), lambda i, j: (0, 0)),
        ],
        out_specs=[
            pl.BlockSpec((SSM_LB, ROWS, LANES), lambda i, j: (0, i, 0)),
            qkv_spec, qkv_spec, qkv_spec,
        ],
        out_shape=[
            jax.ShapeDtypeStruct((SSM_LB, TOKENS, LANES), F32),
            qkv_shape, qkv_shape, qkv_shape,
        ],
        scratch_shapes=[pltpu.VMEM((ROWS, D_MODEL), BF16)],
        compiler_params=pltpu.CompilerParams(
            dimension_semantics=("parallel", "arbitrary"), vmem_limit_bytes=VMEM_LIMIT),
        name="inproj",
    )(x3, shift1, scale1, norm1_w, w_in_bf, q_norm_w, k_norm_w)


def _ssm_prep_kernel(are_ref, aim_ref, ldt_ref, bre_ref, bim_ref, cim_ref,
                     abr_ref, abi_ref, bbr_ref, bbi_ref, ncim_ref):
    a_re = are_ref[...]
    a_im = aim_ref[...]
    dt = jnp.exp(ldt_ref[...])
    mag = jnp.exp(a_re * dt)
    abar_re = mag * jnp.cos(a_im * dt)
    abar_im = mag * jnp.sin(a_im * dt)
    num_re = abar_re - 1.0
    num_im = abar_im
    den = a_re * a_re + a_im * a_im
    z_re = (num_re * a_re + num_im * a_im) / den
    z_im = (num_im * a_re - num_re * a_im) / den
    b_re = bre_ref[...]
    b_im = bim_ref[...]
    abr_ref[...] = abar_re
    abi_ref[...] = abar_im
    bbr_ref[...] = z_re * b_re - z_im * b_im
    bbi_ref[...] = z_re * b_im + z_im * b_re
    ncim_ref[...] = -cim_ref[...]


def _ssm_prep(a_re, a_im, log_dt, b_re, b_im, c_re, c_im):
    G, N, C = SSM_GROUPS, SSM_STATE, SSM_GROUP
    tile_n = lambda a: jnp.tile(a, (1, C))
    cn = lambda b: b.transpose(0, 2, 1).reshape(G, C * N)
    shp = jax.ShapeDtypeStruct((G, C * N), F32)
    abr, abi, bbr, bbi, ncim = pl.pallas_call(
        _ssm_prep_kernel,
        out_shape=[shp] * 5,
        name="ssm_prep",
    )(tile_n(a_re), tile_n(a_im), log_dt.reshape(G, 1), cn(b_re), cn(b_im),
      c_im.reshape(G, C * N))

    eye = jnp.eye(16, dtype=F32)
    a_all = jnp.stack([abr[:, :N], abi[:, :N]])
    a_all = a_all.reshape(2, SSM_COLSETS, 16 * N).transpose(1, 0, 2)
    a_all = a_all.reshape(SSM_COLSETS, 1, 2 * COLSET_ST)
    bb = jnp.stack([bbr, bbi]).reshape(2, SSM_COLSETS, 16, C, N)
    b_exp = jnp.einsum('psgcn,gh->sgcphn', bb, eye)
    b_exp = b_exp.reshape(SSM_COLSETS, COLSET_CH, 2 * COLSET_ST).astype(BF16)
    cc = jnp.stack([c_re.reshape(G, C * N), ncim]).reshape(2, SSM_COLSETS, 16, C, N)
    c_exp = jnp.einsum('psgcn,gh->sphngc', cc, eye)
    c_exp = c_exp.reshape(SSM_COLSETS, 2 * COLSET_ST, COLSET_CH).astype(BF16)
    return a_all, b_exp, c_exp


def _cmul(ar, ai, br, bi):
    return ar * br - ai * bi, ar * bi + ai * br


def _ssm_scan(bu_scr, ar, ai, hr, hi, store):
    def body(t, carry):
        hr, hi = carry
        row = pl.multiple_of(t * N_SEG, N_SEG)
        br = bu_scr[pl.ds(row, N_SEG), :COLSET_ST]
        bi = bu_scr[pl.ds(row, N_SEG), COLSET_ST:]
        nr = ar * hr - ai * hi + br
        ni = ar * hi + ai * hr + bi
        if store:
            bu_scr[pl.ds(row, N_SEG), :COLSET_ST] = nr
            bu_scr[pl.ds(row, N_SEG), COLSET_ST:] = ni
        return nr, ni

    return lax.fori_loop(0, ROW_TAU, body, (hr, hi), unroll=4)


def _colset_u(u_ref):
    return jnp.concatenate([u_ref[0], u_ref[1]], axis=1)


def _ssm_a_kernel(u_ref, b_ref, a_ref, hfin_ref, bu_scr, h_scr):
    ch = pl.program_id(1)

    @pl.when(ch == 0)
    def _():
        h_scr[...] = jnp.zeros_like(h_scr)

    bu_scr[...] = jnp.dot(_colset_u(u_ref).astype(BF16), b_ref[0], preferred_element_type=F32)
    a = a_ref[0]
    ar = jnp.broadcast_to(a[:, :COLSET_ST], (N_SEG, COLSET_ST))
    ai = jnp.broadcast_to(a[:, COLSET_ST:], (N_SEG, COLSET_ST))
    hr, hi = _ssm_scan(bu_scr, ar, ai, h_scr[:, :COLSET_ST], h_scr[:, COLSET_ST:], False)
    h_scr[:, :COLSET_ST] = hr
    h_scr[:, COLSET_ST:] = hi

    @pl.when(ch == N_ROWBLK - 1)
    def _():
        hfin_ref[0] = h_scr[...]


def _ssm_b_kernel(u_ref, b_ref, c_ref, a_ref, d_ref, hfin_ref, y_ref, bu_scr, h_scr):
    ch = pl.program_id(1)
    a = a_ref[0]
    ar = jnp.broadcast_to(a[:, :COLSET_ST], (N_SEG, COLSET_ST))
    ai = jnp.broadcast_to(a[:, COLSET_ST:], (N_SEG, COLSET_ST))

    @pl.when(ch == 0)
    def _():
        f = hfin_ref[0]
        fr, fi = f[:, :COLSET_ST], f[:, COLSET_ST:]
        pr, pi = ar, ai
        for _ in range(SEG_LEN.bit_length() - 1):
            pr, pi = _cmul(pr, pi, pr, pi)
        q = lax.broadcasted_iota(jnp.int32, (N_SEG, COLSET_ST), 0) % SEG_PER_BATCH

        def shifted(v, k):
            return jnp.where(q >= k, pltpu.roll(v, k, 0), 0.0)

        ir, ii = shifted(fr, 1), shifted(fi, 1)
        wr, wi = pr, pi
        for k in range(2, SEG_PER_BATCH):
            tr, ti = _cmul(wr, wi, shifted(fr, k), shifted(fi, k))
            ir, ii = ir + tr, ii + ti
            wr, wi = _cmul(wr, wi, pr, pi)
        h_scr[:, :COLSET_ST] = ir
        h_scr[:, COLSET_ST:] = ii

    u = _colset_u(u_ref)
    bu_scr[...] = jnp.dot(u.astype(BF16), b_ref[0], preferred_element_type=F32)
    hr, hi = _ssm_scan(bu_scr, ar, ai, h_scr[:, :COLSET_ST], h_scr[:, COLSET_ST:], True)
    h_scr[:, :COLSET_ST] = hr
    h_scr[:, COLSET_ST:] = hi
    y = jnp.dot(bu_scr[...].astype(BF16), c_ref[0], preferred_element_type=F32)
    y = jax.nn.gelu(y + d_ref[...] * u)
    y_ref[0] = y[:, :LANES]
    y_ref[1] = y[:, LANES:]


def _ssm(u_il, a_all, b_exp, c_exp, d_flat):
    u_spec = pl.BlockSpec((2, ROWS, LANES), lambda s, c: (s, c, 0))
    a_spec = pl.BlockSpec((1, 1, 2 * COLSET_ST), lambda s, c: (s, 0, 0))
    b_spec = pl.BlockSpec((1, COLSET_CH, 2 * COLSET_ST), lambda s, c: (s, 0, 0))
    hfin_spec = pl.BlockSpec((1, N_SEG, 2 * COLSET_ST), lambda s, c: (s, 0, 0))
    scratch = [pltpu.VMEM((ROWS, 2 * COLSET_ST), F32), pltpu.VMEM((N_SEG, 2 * COLSET_ST), F32)]
    params = pltpu.CompilerParams(
        dimension_semantics=("parallel", "arbitrary"), vmem_limit_bytes=VMEM_LIMIT)
    hfin = pl.pallas_call(
        _ssm_a_kernel,
        grid=(SSM_COLSETS, N_ROWBLK),
        in_specs=[u_spec, b_spec, a_spec],
        out_specs=hfin_spec,
        out_shape=jax.ShapeDtypeStruct((SSM_COLSETS, N_SEG, 2 * COLSET_ST), F32),
        scratch_shapes=scratch,
        compiler_params=params,
        name="ssm_a",
    )(u_il, b_exp, a_all)
    return pl.pallas_call(
        _ssm_b_kernel,
        grid=(SSM_COLSETS, N_ROWBLK),
        in_specs=[
            u_spec, b_spec,
            pl.BlockSpec((1, 2 * COLSET_ST, COLSET_CH), lambda s, c: (s, 0, 0)),
            a_spec,
            pl.BlockSpec((1, COLSET_CH), lambda s, c: (0, s)),
            hfin_spec,
        ],
        out_specs=u_spec,
        out_shape=jax.ShapeDtypeStruct((SSM_LB, TOKENS, LANES), F32),
        scratch_shapes=scratch,
        compiler_params=params,
        name="ssm_b",
    )(u_il, b_exp, c_exp, a_all, d_flat, hfin)


def _branch_distances():
    out = []
    for dil in BRANCH_DILATIONS:
        per = 32 if dil == 1 else BRANCH_BLOCK
        idx = np.arange(BRANCH_BLOCK)
        pos = (BRANCH_BLOCK // per) * (idx % per) + idx // per
        dist = np.concatenate([pos[:, None] - pos[None, :] + BRANCH_BLOCK,
                               pos[:, None] - pos[None, :]], axis=1)
        ok = (dist >= 0) & (dist <= BRANCH_BLOCK)
        out.append(np.where(ok, dil * dist, -1).astype(np.float32))
    return np.stack(out)


def _attn_tile(qt, kt, vt, bm):
    s = lax.dot_general(qt, kt, (((1,), (1,)), ((), ())), preferred_element_type=F32)
    s = jnp.where(bm > 0.5 * MASK_VALUE, s + bm, MASK_VALUE)
    m = jnp.max(s, axis=-1, keepdims=True)
    p = jnp.exp(s - m)
    l = jnp.sum(p, axis=-1, keepdims=True)
    o = jnp.dot(p.astype(BF16), vt, preferred_element_type=F32)
    return m, l, o


def _attn_kernel(q_ref, k_ref, v_ref, dd_ref, o_ref,
                 nat_scr, q4, k4, v4, m4, l4, o4, bias_scr):
    h = pl.program_id(1)
    i = pl.program_id(2)
    Q = ATT_Q
    E = HEAD_DIM
    BLK = BRANCH_BLOCK
    cur = pl.multiple_of((i % 2) * Q, Q)
    prv = pl.multiple_of(Q - (i % 2) * Q, Q)

    def to_slabs(src_ref, dst, base):
        nat_scr[...] = src_ref[0, 0].astype(F32)
        for r in range(4):
            dst[r, pl.ds(base, Q), :] = nat_scr[pl.ds(r, Q, stride=4), :]

    to_slabs(q_ref, q4, 0)
    to_slabs(k_ref, k4, cur)
    to_slabs(v_ref, v4, cur)

    @pl.when(i == 0)
    def _():
        for r in range(4):
            k4[r, pl.ds(prv, Q), :] = jnp.zeros((Q, E), F32)
            v4[r, pl.ds(prv, Q), :] = jnp.zeros((Q, E), F32)

    hv = jnp.full((BLK, 2 * BLK), h, jnp.int32).astype(F32)
    slope = jnp.exp2(-8.0 * (hv + 1.0) / N_HEADS)
    is_prev = lax.broadcasted_iota(jnp.int32, (BLK, 2 * BLK), 1) < BLK
    for br in range(3):
        dd = dd_ref[br]
        bm = jnp.where(dd >= 0.0, (-slope) * dd, MASK_VALUE)
        bias_scr[2 * br] = bm
        bias_scr[2 * br + 1] = jnp.where(is_prev, MASK_VALUE, bm)

    def load(scr, chunks):
        return jnp.concatenate([scr[c] for c in chunks], axis=0)

    def merge(chunks, nrows, m_t, l_t, o_t, first, last):
        for n, c in enumerate(chunks):
            sl = slice(n * nrows, (n + 1) * nrows)
            m_c = jnp.broadcast_to(m_t[sl], (nrows, E))
            l_c = jnp.broadcast_to(l_t[sl], (nrows, E))
            o_c = o_t[sl]
            if first:
                m4[c] = m_c
                l4[c] = l_c
                o4[c] = o_c
                continue
            m_old = m4[c]
            m_new = jnp.maximum(m_old, m_c)
            alpha = jnp.exp(m_old - m_new)
            beta = jnp.exp(m_c - m_new)
            l_new = alpha * l4[c] + beta * l_c
            o_new = alpha * o4[c] + beta * o_c
            if last:
                o4[c] = o_new / l_new
            else:
                m4[c] = m_new
                l4[c] = l_new
                o4[c] = o_new

    def run_tile(br, q_chunks, kp_chunks, kc_chunks, nrows, seq_start):
        qt = load(q4, q_chunks).astype(BF16)
        kt = jnp.concatenate([load(k4, kp_chunks), load(k4, kc_chunks)], axis=0).astype(BF16)
        vt = jnp.concatenate([load(v4, kp_chunks), load(v4, kc_chunks)], axis=0).astype(BF16)
        if seq_start:
            bm = bias_scr[2 * br + (i == 0).astype(jnp.int32)]
        else:
            bm = bias_scr[2 * br]
        m_t, l_t, o_t = _attn_tile(qt, kt, vt, bm)
        merge(q_chunks, nrows, m_t, l_t, o_t, br == 0, br == 2)

    for r in range(4):
        for r2 in range(4):
            run_tile(0,
                     [(r, pl.ds(r2, BLK, stride=4), slice(None))],
                     [(r, pl.ds(prv + r2, BLK, stride=4), slice(None))],
                     [(r, pl.ds(cur + r2, BLK, stride=4), slice(None))],
                     BLK, True)

    for r in range(4):
        for b in range(4):
            kp = prv + (Q - BLK) if b == 0 else cur + (b - 1) * BLK
            run_tile(1,
                     [(r, pl.ds(b * BLK, BLK), slice(None))],
                     [(r, pl.ds(pl.multiple_of(kp, BLK), BLK), slice(None))],
                     [(r, pl.ds(pl.multiple_of(cur + b * BLK, BLK), BLK), slice(None))],
                     BLK, b == 0)

    for b in range(SEG_LEN // BLK):
        kp = prv + (Q - 32) if b == 0 else cur + (b - 1) * 32
        run_tile(2,
                 [(r, pl.ds(b * 32, 32), slice(None)) for r in range(4)],
                 [(r, pl.ds(pl.multiple_of(kp, 32), 32), slice(None)) for r in range(4)],
                 [(r, pl.ds(pl.multiple_of(cur + b * 32, 32), 32), slice(None))
                  for r in range(4)],
                 32, b == 0)

    for r in range(4):
        nat_scr[pl.ds(r, Q, stride=4), :] = o4[r]
    o_ref[0, 0] = nat_scr[...].astype(BF16)


def _attention(q, k, v):
    blk = pl.BlockSpec((1, 1, SEG_LEN, HEAD_DIM),
                       lambda b, h, i: (b * SEG_PER_BATCH + i, h, 0, 0))
    dd = jnp.asarray(_branch_distances())
    slab = pltpu.VMEM((4, ATT_Q, HEAD_DIM), F32)
    slab2 = pltpu.VMEM((4, 2 * ATT_Q, HEAD_DIM), F32)
    return pl.pallas_call(
        _attn_kernel,
        grid=(BATCH, N_HEADS, SEG_PER_BATCH),
        in_specs=[blk, blk, blk,
                  pl.BlockSpec((3, BRANCH_BLOCK, 2 * BRANCH_BLOCK), lambda b, h, i: (0, 0, 0))],
        out_specs=blk,
        out_shape=jax.ShapeDtypeStruct((N_SEG, N_HEADS, SEG_LEN, HEAD_DIM), BF16),
        scratch_shapes=[
            pltpu.VMEM((SEG_LEN, HEAD_DIM), F32),
            slab, slab2, slab2,
            slab, slab, slab,
            pltpu.VMEM((6, BRANCH_BLOCK, 2 * BRANCH_BLOCK), F32),
        ],
        compiler_params=pltpu.CompilerParams(
            dimension_semantics=("parallel", "parallel", "arbitrary"),
            vmem_limit_bytes=VMEM_LIMIT),
        name="attn",
    )(q, k, v, dd)


def _mix_kernel(x_ref, y_ref, att_ref, g_ref, wglu_ref, wout_ref, o_ref, cat_scr):
    y = jnp.concatenate(
        [jnp.concatenate([y_ref[lb, pl.ds(seg, ROW_TAU, stride=N_SEG), :]
                          for lb in range(SSM_LB)], axis=1)
         for seg in range(N_SEG)], axis=0)
    gl = jnp.dot(y.astype(BF16), wglu_ref[...], preferred_element_type=F32)
    cat_scr[:, :SSM_WIDTH] = (y * _sigmoid(gl)).astype(BF16)
    for seg in range(N_SEG):
        for h in range(N_HEADS):
            c0 = SSM_WIDTH + h * HEAD_DIM
            cat_scr[_seg_rows(seg), c0:c0 + HEAD_DIM] = att_ref[seg, h]
    mixed = jnp.dot(cat_scr[...], wout_ref[...], preferred_element_type=F32)
    x = x_ref[...].reshape(ROWS, D_MODEL)
    out = jnp.concatenate([x[rows] + g_ref[b] * mixed[rows] for rows, b in _HALVES], axis=0)
    o_ref[...] = out.reshape(N_SEG, ROW_TAU, D_MODEL)


def _mix(x3, y_il, att, gate1, w_glu_bf, w_out_bf):
    x_spec = pl.BlockSpec((N_SEG, ROW_TAU, D_MODEL), lambda i: (0, i, 0))
    return pl.pallas_call(
        _mix_kernel,
        grid=(N_ROWBLK,),
        in_specs=[
            x_spec,
            pl.BlockSpec((SSM_LB, ROWS, LANES), lambda i: (0, i, 0)),
            pl.BlockSpec((N_SEG, N_HEADS, ROW_TAU, HEAD_DIM), lambda i: (0, 0, i, 0)),
            pl.BlockSpec((BATCH, 1, D_MODEL), lambda i: (0, 0, 0)),
            pl.BlockSpec((SSM_WIDTH, SSM_WIDTH), lambda i: (0, 0)),
            pl.BlockSpec((D_MODEL, D_MODEL), lambda i: (0, 0)),
        ],
        out_specs=x_spec,
        out_shape=jax.ShapeDtypeStruct((N_SEG, SEG_LEN, D_MODEL), F32),
        scratch_shapes=[pltpu.VMEM((ROWS, D_MODEL), BF16)],
        compiler_params=pltpu.CompilerParams(
            dimension_semantics=("parallel",), vmem_limit_bytes=VMEM_LIMIT),
        name="mix",
    )(x3, y_il, att, gate1, w_glu_bf, w_out_bf)


def _ffn_kernel(x_ref, sh_ref, sc_ref, g_ref, nw_ref, wg_ref, wu_ref, wd_ref, o_ref, h_scr):
    j = pl.program_id(1)

    @pl.when(j == 0)
    def _():
        _modulated_norm(x_ref[...], nw_ref[...], sh_ref, sc_ref, h_scr,
                        ((slice(None), 0),))
        o_ref[...] = jnp.zeros_like(o_ref)

    h = h_scr[...]
    g = jnp.dot(h, wg_ref[...], preferred_element_type=F32)
    u = jnp.dot(h, wu_ref[...], preferred_element_type=F32)
    a = (g * _sigmoid(g) * u).astype(BF16)
    o_ref[...] += jnp.dot(a, wd_ref[...], preferred_element_type=F32)

    @pl.when(j == FFN_HIDDEN // FFN_TH - 1)
    def _():
        o_ref[...] = x_ref[...] + g_ref[0] * o_ref[...]


def _ffn(x1, shift2, scale2, gate2, norm2_w, wg_bf, wu_bf, wd_bf):
    tm, th = FFN_TM, FFN_TH
    per_batch = SEQ // tm
    mod_spec = pl.BlockSpec((1, 1, D_MODEL), lambda i, j: (i // per_batch, 0, 0))
    return pl.pallas_call(
        _ffn_kernel,
        grid=(TOKENS // tm, FFN_HIDDEN // th),
        in_specs=[
            pl.BlockSpec((tm, D_MODEL), lambda i, j: (i, 0)),
            mod_spec, mod_spec, mod_spec,
            pl.BlockSpec((1, D_MODEL), lambda i, j: (0, 0)),
            pl.BlockSpec((D_MODEL, th), lambda i, j: (0, j)),
            pl.BlockSpec((D_MODEL, th), lambda i, j: (0, j)),
            pl.BlockSpec((th, D_MODEL), lambda i, j: (j, 0)),
        ],
        out_specs=pl.BlockSpec((tm, D_MODEL), lambda i, j: (i, 0)),
        out_shape=jax.ShapeDtypeStruct((TOKENS, D_MODEL), F32),
        scratch_shapes=[pltpu.VMEM((tm, D_MODEL), BF16)],
        compiler_params=pltpu.CompilerParams(
            dimension_semantics=("parallel", "arbitrary"), vmem_limit_bytes=VMEM_LIMIT),
        name="ffn",
    )(x1, shift2, scale2, gate2, norm2_w, wg_bf, wu_bf, wd_bf)


def kernel(x, c, w_ada, b_ada, norm1_w, w_in, ssm_a_re, ssm_a_im, ssm_log_dt, ssm_b_re, ssm_b_im,
           ssm_c_re, ssm_c_im, ssm_d, ssm_w_glu, q_norm_w, k_norm_w, w_out, norm2_w,
           w_ffn_gate, w_ffn_up, w_ffn_down):
    x3 = x.reshape(N_SEG, SEG_LEN, D_MODEL)
    for i in range(w_ada.shape[0]):
        mod = _ada(c, w_ada[i], b_ada[i])
        shift1, scale1, gate1, shift2, scale2, gate2 = [
            m.reshape(BATCH, 1, D_MODEL) for m in jnp.split(mod, N_MOD, axis=-1)]

        u_il, q, k, v = _inproj(x3, shift1, scale1, norm1_w[i].reshape(1, D_MODEL),
                                w_in[i].astype(BF16),
                                q_norm_w[i].reshape(1, HEAD_DIM), k_norm_w[i].reshape(1, HEAD_DIM))

        a_all, b_exp, c_exp = _ssm_prep(ssm_a_re[i], ssm_a_im[i], ssm_log_dt[i], ssm_b_re[i],
                                        ssm_b_im[i], ssm_c_re[i], ssm_c_im[i])
        y_il = _ssm(u_il, a_all, b_exp, c_exp, ssm_d[i].reshape(1, SSM_WIDTH))
        att = _attention(q, k, v)

        x3 = _mix(x3, y_il, att, gate1, ssm_w_glu[i].astype(BF16), w_out[i].astype(BF16))
        x2 = _ffn(x3.reshape(TOKENS, D_MODEL), shift2, scale2, gate2,
                  norm2_w[i].reshape(1, D_MODEL), w_ffn_gate[i].astype(BF16),
                  w_ffn_up[i].astype(BF16), w_ffn_down[i].astype(BF16))
        x3 = x2.reshape(N_SEG, SEG_LEN, D_MODEL)
    return x3.reshape(BATCH, SEQ, D_MODEL)
```

```python
import numpy as np
import jax
import jax.numpy as jnp
from jax import lax
from jax.experimental import pallas as pl
from jax.experimental.pallas import tpu as pltpu

F32 = jnp.float32
BF16 = jnp.bfloat16

D_MODEL = 2048
BATCH = 2
SEQ = 8192
TOKENS = BATCH * SEQ
SSM_WIDTH = 1024
SSM_GROUP = 16
SSM_GROUPS = 64
SSM_STATE = 64
ATTN_WIDTH = 1024
HEAD_DIM = 128
N_HEADS = 8
BRANCH_DILATIONS = (16, 4, 1)
BRANCH_BLOCK = 128
FFN_HIDDEN = 5632
N_MOD = 6
NORM_EPS = 1e-6
MASK_VALUE = -1e30
LANES = 128

N_SEG = 8
SEG_LEN = TOKENS // N_SEG
SEG_PER_BATCH = SEQ // SEG_LEN
ROW_TAU = 64
ROWS = N_SEG * ROW_TAU
N_ROWBLK = SEG_LEN // ROW_TAU
SSM_LB = SSM_WIDTH // LANES
SSM_COLSETS = 4
COLSET_CH = SSM_WIDTH // SSM_COLSETS
COLSET_ST = 16 * SSM_STATE
SSM_TAU = 64
SSM_ROWS = N_SEG * SSM_TAU
SSM_NCHUNK = SEG_LEN // SSM_TAU
SSM_SUB_TAU = 16
SSM_NSUB = SSM_TAU // SSM_SUB_TAU

ATT_Q = SEG_LEN // 4

ADA_TN = 1024
FFN_TM = 512
FFN_TH = 512

VMEM_LIMIT = 56 * 1024 * 1024


def _sigmoid(x):
    return 1.0 / (1.0 + jnp.exp(-x))


def _modulated_norm(x, nw, sh_ref, sc_ref, out_scr, row_batches):
    ms = jnp.mean(x * x, axis=-1, keepdims=True)
    y = x * lax.rsqrt(ms + NORM_EPS) * nw
    for rows, b in row_batches:
        out_scr[rows, :] = (y[rows] * (1.0 + sc_ref[b]) + sh_ref[b]).astype(BF16)


_HALVES = ((slice(0, ROWS // 2), 0), (slice(ROWS // 2, ROWS), 1))


def _ada_kernel(ct_ref, w_ref, b_ref, o_ref, cond_scr):
    c = ct_ref[...]
    cond_scr[...] = c * _sigmoid(c)

    def body(kc, acc):
        row = pl.multiple_of(kc * 8, 8)
        wk = w_ref[pl.ds(row, 8), :]
        ck = cond_scr[pl.ds(row, 8), :]
        return tuple(acc[b] + ck[:, b:b + 1] * wk for b in range(BATCH))

    zero = jnp.zeros((8, ADA_TN), F32)
    acc = lax.fori_loop(0, D_MODEL // 8, body, (zero,) * BATCH, unroll=8)
    rows = [jnp.sum(a, axis=0, keepdims=True) for a in acc]
    o_ref[...] = jnp.concatenate(rows, axis=0) + b_ref[...]


def _ada(c, w_ada, b_ada):
    n = N_MOD * D_MODEL
    return pl.pallas_call(
        _ada_kernel,
        grid=(n // ADA_TN,),
        in_specs=[
            pl.BlockSpec((D_MODEL, BATCH), lambda j: (0, 0)),
            pl.BlockSpec((D_MODEL, ADA_TN), lambda j: (0, j)),
            pl.BlockSpec((1, ADA_TN), lambda j: (0, j)),
        ],
        out_specs=pl.BlockSpec((BATCH, ADA_TN), lambda j: (0, j)),
        out_shape=jax.ShapeDtypeStruct((BATCH, n), F32),
        scratch_shapes=[pltpu.VMEM((D_MODEL, BATCH), F32)],
        compiler_params=pltpu.CompilerParams(
            dimension_semantics=("arbitrary",), vmem_limit_bytes=VMEM_LIMIT),
        name="ada",
    )(c.T, w_ada, b_ada.reshape(1, n))


def _head_rms(blk, w):
    ms = jnp.mean(blk * blk, axis=-1, keepdims=True)
    return blk * lax.rsqrt(ms + NORM_EPS) * w


def _seg_rows(seg):
    return slice(seg * ROW_TAU, (seg + 1) * ROW_TAU)


def _inproj_kernel(x_ref, sh_ref, sc_ref, nw_ref, w_ref, qw_ref, kw_ref,
                   u_ref, q_ref, k_ref, v_ref, h_scr):
    x = x_ref[...].reshape(ROWS, D_MODEL)
    _modulated_norm(x, nw_ref[...], sh_ref, sc_ref, h_scr, _HALVES)
    h = h_scr[...]

    def proj(n):
        return jnp.dot(h, w_ref[:, n * 1024:(n + 1) * 1024], preferred_element_type=F32)

    def per_head(res, out_ref, fn):
        for hd in range(N_HEADS):
            blk = fn(res[:, hd * HEAD_DIM:(hd + 1) * HEAD_DIM]).astype(BF16)
            for seg in range(N_SEG):
                out_ref[seg, hd] = blk[_seg_rows(seg)]

    res = proj(0)
    for lb in range(SSM_LB):
        for seg in range(N_SEG):
            u_ref[lb, pl.ds(seg, ROW_TAU, stride=N_SEG), :] = (
                res[_seg_rows(seg), lb * LANES:(lb + 1) * LANES])
    per_head(proj(1), q_ref, lambda blk: _head_rms(blk, qw_ref[...]) * (HEAD_DIM ** -0.5))
    per_head(proj(2), k_ref, lambda blk: _head_rms(blk, kw_ref[...]))
    per_head(proj(3), v_ref, lambda blk: blk)


def _inproj(x3, shift1, scale1, norm1_w, w_in_bf, q_norm_w, k_norm_w):
    qkv_shape = jax.ShapeDtypeStruct((N_SEG, N_HEADS, SEG_LEN, HEAD_DIM), BF16)
    qkv_spec = pl.BlockSpec((N_SEG, N_HEADS, ROW_TAU, HEAD_DIM), lambda i: (0, 0, i, 0))
    mod_spec = pl.BlockSpec((BATCH, 1, D_MODEL), lambda i: (0, 0, 0))
    return pl.pallas_call(
        _inproj_kernel,
        grid=(N_ROWBLK,),
        in_specs=[
            pl.BlockSpec((N_SEG, ROW_TAU, D_MODEL), lambda i: (0, i, 0)),
            mod_spec, mod_spec,
            pl.BlockSpec((1, D_MODEL), lambda i: (0, 0)),
            pl.BlockSpec((D_MODEL, 4 * 1024), lambda i: (0, 0), pipeline_mode=pl.Buffered(1)),
            pl.BlockSpec((1, HEAD_DIM), lambda i: (0, 0)),
            pl.BlockSpec((1, HEAD_DIM), lambda i: (0, 0)),
        ],
        out_specs=[
            pl.BlockSpec((SSM_LB, ROWS, LANES), lambda i: (0, i, 0)),
            qkv_spec, qkv_spec, qkv_spec,
        ],
        out_shape=[
            jax.ShapeDtypeStruct((SSM_LB, TOKENS, LANES), F32),
            qkv_shape, qkv_shape, qkv_shape,
        ],
        scratch_shapes=[pltpu.VMEM((ROWS, D_MODEL), BF16)],
        compiler_params=pltpu.CompilerParams(
            dimension_semantics=("parallel",), vmem_limit_bytes=VMEM_LIMIT),
        name="inproj",
    )(x3, shift1, scale1, norm1_w, w_in_bf, q_norm_w, k_norm_w)


def _ssm_prep_kernel(are_ref, aim_ref, ldt_ref, bre_ref, bim_ref, cim_ref,
                     abr_ref, abi_ref, bbr_ref, bbi_ref, ncim_ref):
    a_re = are_ref[...]
    a_im = aim_ref[...]
    dt = jnp.exp(ldt_ref[...])
    mag = jnp.exp(a_re * dt)
    abar_re = mag * jnp.cos(a_im * dt)
    abar_im = mag * jnp.sin(a_im * dt)
    num_re = abar_re - 1.0
    num_im = abar_im
    den = a_re * a_re + a_im * a_im
    z_re = (num_re * a_re + num_im * a_im) / den
    z_im = (num_im * a_re - num_re * a_im) / den
    b_re = bre_ref[...]
    b_im = bim_ref[...]
    abr_ref[...] = abar_re
    abi_ref[...] = abar_im
    bbr_ref[...] = z_re * b_re - z_im * b_im
    bbi_ref[...] = z_re * b_im + z_im * b_re
    ncim_ref[...] = -cim_ref[...]


def _ssm_prep(a_re, a_im, log_dt, b_re, b_im, c_re, c_im):
    G, N, C = SSM_GROUPS, SSM_STATE, SSM_GROUP
    tile_n = lambda a: jnp.tile(a, (1, C))
    cn = lambda b: b.transpose(0, 2, 1).reshape(G, C * N)
    shp = jax.ShapeDtypeStruct((G, C * N), F32)
    abr, abi, bbr, bbi, ncim = pl.pallas_call(
        _ssm_prep_kernel,
        out_shape=[shp] * 5,
        name="ssm_prep",
    )(tile_n(a_re), tile_n(a_im), log_dt.reshape(G, 1), cn(b_re), cn(b_im),
      c_im.reshape(G, C * N))

    eye = jnp.eye(16, dtype=F32)
    a_all = jnp.stack([abr[:, :N], abi[:, :N]])
    a_all = a_all.reshape(2, SSM_COLSETS, 16 * N).transpose(1, 0, 2)
    a_all = a_all.reshape(SSM_COLSETS, 1, 2 * COLSET_ST)
    bb = jnp.stack([bbr, bbi]).reshape(2, SSM_COLSETS, 16, C, N)
    b_exp = jnp.einsum('psgcn,gh->sgcphn', bb, eye)
    b_exp = b_exp.reshape(SSM_COLSETS, COLSET_CH, 2 * COLSET_ST).astype(BF16)
    cc = jnp.stack([c_re.reshape(G, C * N), ncim]).reshape(2, SSM_COLSETS, 16, C, N)
    c_exp = jnp.einsum('psgcn,gh->sphngc', cc, eye)
    c_exp = c_exp.reshape(SSM_COLSETS, 2 * COLSET_ST, COLSET_CH).astype(BF16)
    return a_all, b_exp, c_exp


def _cmul(ar, ai, br, bi):
    return ar * br - ai * bi, ar * bi + ai * br


def _ssm_scan(bu_scr, ar, ai, hr, hi, t0, store):
    for t in range(t0, t0 + SSM_SUB_TAU):
        rows = slice(t * N_SEG, (t + 1) * N_SEG)
        br = bu_scr[rows, :COLSET_ST]
        bi = bu_scr[rows, COLSET_ST:]
        hr, hi = ar * hr - ai * hi + br, ar * hi + ai * hr + bi
        if store:
            bu_scr[rows, :COLSET_ST] = hr
            bu_scr[rows, COLSET_ST:] = hi
    return hr, hi


def _ssm_pipeline(u_bf, b_ref, bu_scr, h_scr, ar, ai, store, after_scan):
    def bu(k):
        rows = _sub_rows(k)
        bu_scr[rows, :] = jnp.dot(u_bf[rows], b_ref[0], preferred_element_type=F32)

    hr, hi = h_scr[:, :COLSET_ST], h_scr[:, COLSET_ST:]
    bu(0)
    for k in range(SSM_NSUB):
        if k + 1 < SSM_NSUB:
            bu(k + 1)
        hr, hi = _ssm_scan(bu_scr, ar, ai, hr, hi, k * SSM_SUB_TAU, store)
        after_scan(k)
    h_scr[:, :COLSET_ST] = hr
    h_scr[:, COLSET_ST:] = hi


def _sub_rows(k):
    return slice(k * SSM_SUB_TAU * N_SEG, (k + 1) * SSM_SUB_TAU * N_SEG)


def _colset_u(u_ref):
    return jnp.concatenate([u_ref[0], u_ref[1]], axis=1)


def _ssm_a_kernel(u_ref, b_ref, a_ref, hfin_ref, bu_scr, h_scr):
    ch = pl.program_id(1)

    @pl.when(ch == 0)
    def _():
        h_scr[...] = jnp.zeros_like(h_scr)

    a = a_ref[0]
    ar = jnp.broadcast_to(a[:, :COLSET_ST], (N_SEG, COLSET_ST))
    ai = jnp.broadcast_to(a[:, COLSET_ST:], (N_SEG, COLSET_ST))
    _ssm_pipeline(_colset_u(u_ref).astype(BF16), b_ref, bu_scr, h_scr, ar, ai, False,
                  lambda k: None)

    @pl.when(ch == SSM_NCHUNK - 1)
    def _():
        hfin_ref[0] = h_scr[...]


def _ssm_b_kernel(u_ref, b_ref, c_ref, a_ref, d_ref, hfin_ref, y_ref, bu_scr, h_scr):
    ch = pl.program_id(1)
    a = a_ref[0]
    ar = jnp.broadcast_to(a[:, :COLSET_ST], (N_SEG, COLSET_ST))
    ai = jnp.broadcast_to(a[:, COLSET_ST:], (N_SEG, COLSET_ST))

    @pl.when(ch == 0)
    def _():
        f = hfin_ref[0]
        fr, fi = f[:, :COLSET_ST], f[:, COLSET_ST:]
        pr, pi = ar, ai
        for _ in range(SEG_LEN.bit_length() - 1):
            pr, pi = _cmul(pr, pi, pr, pi)
        q = lax.broadcasted_iota(jnp.int32, (N_SEG, COLSET_ST), 0) % SEG_PER_BATCH

        def shifted(v, k):
            return jnp.where(q >= k, pltpu.roll(v, k, 0), 0.0)

        ir, ii = shifted(fr, 1), shifted(fi, 1)
        wr, wi = pr, pi
        for k in range(2, SEG_PER_BATCH):
            tr, ti = _cmul(wr, wi, shifted(fr, k), shifted(fi, k))
            ir, ii = ir + tr, ii + ti
            wr, wi = _cmul(wr, wi, pr, pi)
        h_scr[:, :COLSET_ST] = ir
        h_scr[:, COLSET_ST:] = ii

    u = _colset_u(u_ref)

    def emit_y(k):
        rows = _sub_rows(k)
        y = jnp.dot(bu_scr[rows, :].astype(BF16), c_ref[0], preferred_element_type=F32)
        y = jax.nn.gelu(y + d_ref[...] * u[rows])
        y_ref[0, rows, :] = y[:, :LANES]
        y_ref[1, rows, :] = y[:, LANES:]

    _ssm_pipeline(u.astype(BF16), b_ref, bu_scr, h_scr, ar, ai, True, emit_y)


def _ssm(u_il, a_all, b_exp, c_exp, d_flat):
    u_spec = pl.BlockSpec((2, SSM_ROWS, LANES), lambda s, c: (s, c, 0))
    a_spec = pl.BlockSpec((1, 1, 2 * COLSET_ST), lambda s, c: (s, 0, 0))
    b_spec = pl.BlockSpec((1, COLSET_CH, 2 * COLSET_ST), lambda s, c: (s, 0, 0))
    hfin_spec = pl.BlockSpec((1, N_SEG, 2 * COLSET_ST), lambda s, c: (s, 0, 0))
    scratch = [pltpu.VMEM((SSM_ROWS, 2 * COLSET_ST), F32),
               pltpu.VMEM((N_SEG, 2 * COLSET_ST), F32)]
    params = pltpu.CompilerParams(
        dimension_semantics=("parallel", "arbitrary"), vmem_limit_bytes=VMEM_LIMIT)
    hfin = pl.pallas_call(
        _ssm_a_kernel,
        grid=(SSM_COLSETS, SSM_NCHUNK),
        in_specs=[u_spec, b_spec, a_spec],
        out_specs=hfin_spec,
        out_shape=jax.ShapeDtypeStruct((SSM_COLSETS, N_SEG, 2 * COLSET_ST), F32),
        scratch_shapes=scratch,
        compiler_params=params,
        name="ssm_a",
    )(u_il, b_exp, a_all)
    return pl.pallas_call(
        _ssm_b_kernel,
        grid=(SSM_COLSETS, SSM_NCHUNK),
        in_specs=[
            u_spec, b_spec,
            pl.BlockSpec((1, 2 * COLSET_ST, COLSET_CH), lambda s, c: (s, 0, 0)),
            a_spec,
            pl.BlockSpec((1, COLSET_CH), lambda s, c: (0, s)),
            hfin_spec,
        ],
        out_specs=u_spec,
        out_shape=jax.ShapeDtypeStruct((SSM_LB, TOKENS, LANES), F32),
        scratch_shapes=scratch,
        compiler_params=params,
        name="ssm_b",
    )(u_il, b_exp, c_exp, a_all, d_flat, hfin)


def _branch_distances():
    out = []
    for dil in BRANCH_DILATIONS:
        per = 32 if dil == 1 else BRANCH_BLOCK
        idx = np.arange(BRANCH_BLOCK)
        pos = (BRANCH_BLOCK // per) * (idx % per) + idx // per
        dist = np.concatenate([pos[:, None] - pos[None, :] + BRANCH_BLOCK,
                               pos[:, None] - pos[None, :]], axis=1)
        ok = (dist >= 0) & (dist <= BRANCH_BLOCK)
        out.append(np.where(ok, dil * dist, -1).astype(np.float32))
    return np.stack(out)


def _attn_tile(qt, kt, vt, bm):
    s = lax.dot_general(qt, kt, (((1,), (1,)), ((), ())), preferred_element_type=F32)
    s = jnp.where(bm > 0.5 * MASK_VALUE, s + bm, MASK_VALUE)
    m = jnp.max(s, axis=-1, keepdims=True)
    p = jnp.exp(s - m)
    l = jnp.sum(p, axis=-1, keepdims=True)
    o = jnp.dot(p.astype(BF16), vt, preferred_element_type=F32)
    return m, l, o


def _attn_kernel(q_ref, k_ref, v_ref, dd_ref, o_ref,
                 nat_scr, q4, k4, v4, m4, l4, o4, bias_scr):
    h = pl.program_id(1)
    i = pl.program_id(2)
    Q = ATT_Q
    E = HEAD_DIM
    BLK = BRANCH_BLOCK
    cur = pl.multiple_of((i % 2) * Q, Q)
    prv = pl.multiple_of(Q - (i % 2) * Q, Q)

    def to_slabs(src_ref, dst, base):
        nat_scr[...] = src_ref[0, 0].astype(F32)
        for r in range(4):
            dst[r, pl.ds(base, Q), :] = nat_scr[pl.ds(r, Q, stride=4), :]

    to_slabs(q_ref, q4, 0)
    to_slabs(k_ref, k4, cur)
    to_slabs(v_ref, v4, cur)

    @pl.when(i == 0)
    def _():
        for r in range(4):
            k4[r, pl.ds(prv, Q), :] = jnp.zeros((Q, E), F32)
            v4[r, pl.ds(prv, Q), :] = jnp.zeros((Q, E), F32)

    hv = jnp.full((BLK, 2 * BLK), h, jnp.int32).astype(F32)
    slope = jnp.exp2(-8.0 * (hv + 1.0) / N_HEADS)
    is_prev = lax.broadcasted_iota(jnp.int32, (BLK, 2 * BLK), 1) < BLK
    for br in range(3):
        dd = dd_ref[br]
        bm = jnp.where(dd >= 0.0, (-slope) * dd, MASK_VALUE)
        bias_scr[2 * br] = bm
        bias_scr[2 * br + 1] = jnp.where(is_prev, MASK_VALUE, bm)

    def load(scr, chunks):
        return jnp.concatenate([scr[c] for c in chunks], axis=0)

    def merge(chunks, nrows, m_t, l_t, o_t, first, last):
        for n, c in enumerate(chunks):
            sl = slice(n * nrows, (n + 1) * nrows)
            m_c = jnp.broadcast_to(m_t[sl], (nrows, E))
            l_c = jnp.broadcast_to(l_t[sl], (nrows, E))
            o_c = o_t[sl]
            if first:
                m4[c] = m_c
                l4[c] = l_c
                o4[c] = o_c
                continue
            m_old = m4[c]
            m_new = jnp.maximum(m_old, m_c)
            alpha = jnp.exp(m_old - m_new)
            beta = jnp.exp(m_c - m_new)
            l_new = alpha * l4[c] + beta * l_c
            o_new = alpha * o4[c] + beta * o_c
            if last:
                o4[c] = o_new / l_new
            else:
                m4[c] = m_new
                l4[c] = l_new
                o4[c] = o_new

    def run_tile(br, q_chunks, kp_chunks, kc_chunks, nrows, seq_start):
        qt = load(q4, q_chunks).astype(BF16)
        kt = jnp.concatenate([load(k4, kp_chunks), load(k4, kc_chunks)], axis=0).astype(BF16)
        vt = jnp.concatenate([load(v4, kp_chunks), load(v4, kc_chunks)], axis=0).astype(BF16)
        if seq_start:
            bm = bias_scr[2 * br + (i == 0).astype(jnp.int32)]
        else:
            bm = bias_scr[2 * br]
        m_t, l_t, o_t = _attn_tile(qt, kt, vt, bm)
        merge(q_chunks, nrows, m_t, l_t, o_t, br == 0, br == 2)

    for r in range(4):
        for r2 in range(4):
            run_tile(0,
                     [(r, pl.ds(r2, BLK, stride=4), slice(None))],
                     [(r, pl.ds(prv + r2, BLK, stride=4), slice(None))],
                     [(r, pl.ds(cur + r2, BLK, stride=4), slice(None))],
                     BLK, True)

    for r in range(4):
        for b in range(4):
            kp = prv + (Q - BLK) if b == 0 else cur + (b - 1) * BLK
            run_tile(1,
                     [(r, pl.ds(b * BLK, BLK), slice(None))],
                     [(r, pl.ds(pl.multiple_of(kp, BLK), BLK), slice(None))],
                     [(r, pl.ds(pl.multiple_of(cur + b * BLK, BLK), BLK), slice(None))],
                     BLK, b == 0)

    for b in range(SEG_LEN // BLK):
        kp = prv + (Q - 32) if b == 0 else cur + (b - 1) * 32
        run_tile(2,
                 [(r, pl.ds(b * 32, 32), slice(None)) for r in range(4)],
                 [(r, pl.ds(pl.multiple_of(kp, 32), 32), slice(None)) for r in range(4)],
                 [(r, pl.ds(pl.multiple_of(cur + b * 32, 32), 32), slice(None))
                  for r in range(4)],
                 32, b == 0)

    for r in range(4):
        nat_scr[pl.ds(r, Q, stride=4), :] = o4[r]
    o_ref[0, 0] = nat_scr[...].astype(BF16)


def _attention(q, k, v):
    blk = pl.BlockSpec((1, 1, SEG_LEN, HEAD_DIM),
                       lambda b, h, i: (b * SEG_PER_BATCH + i, h, 0, 0))
    dd = jnp.asarray(_branch_distances())
    slab = pltpu.VMEM((4, ATT_Q, HEAD_DIM), F32)
    slab2 = pltpu.VMEM((4, 2 * ATT_Q, HEAD_DIM), F32)
    return pl.pallas_call(
        _attn_kernel,
        grid=(BATCH, N_HEADS, SEG_PER_BATCH),
        in_specs=[blk, blk, blk,
                  pl.BlockSpec((3, BRANCH_BLOCK, 2 * BRANCH_BLOCK), lambda b, h, i: (0, 0, 0))],
        out_specs=blk,
        out_shape=jax.ShapeDtypeStruct((N_SEG, N_HEADS, SEG_LEN, HEAD_DIM), BF16),
        scratch_shapes=[
            pltpu.VMEM((SEG_LEN, HEAD_DIM), F32),
            slab, slab2, slab2,
            slab, slab, slab,
            pltpu.VMEM((6, BRANCH_BLOCK, 2 * BRANCH_BLOCK), F32),
        ],
        compiler_params=pltpu.CompilerParams(
            dimension_semantics=("parallel", "parallel", "arbitrary"),
            vmem_limit_bytes=VMEM_LIMIT),
        name="attn",
    )(q, k, v, dd)


def _mix_kernel(x_ref, y_ref, att_ref, g_ref, wglu_ref, wout_ref, o_ref, cat_scr):
    y = jnp.concatenate(
        [jnp.concatenate([y_ref[lb, pl.ds(seg, ROW_TAU, stride=N_SEG), :]
                          for lb in range(SSM_LB)], axis=1)
         for seg in range(N_SEG)], axis=0)
    gl = jnp.dot(y.astype(BF16), wglu_ref[...], preferred_element_type=F32)
    cat_scr[:, :SSM_WIDTH] = (y * _sigmoid(gl)).astype(BF16)
    for seg in range(N_SEG):
        for h in range(N_HEADS):
            c0 = SSM_WIDTH + h * HEAD_DIM
            cat_scr[_seg_rows(seg), c0:c0 + HEAD_DIM] = att_ref[seg, h]
    mixed = jnp.dot(cat_scr[...], wout_ref[...], preferred_element_type=F32)
    x = x_ref[...].reshape(ROWS, D_MODEL)
    out = jnp.concatenate([x[rows] + g_ref[b] * mixed[rows] for rows, b in _HALVES], axis=0)
    o_ref[...] = out.reshape(N_SEG, ROW_TAU, D_MODEL)


def _mix(x3, y_il, att, gate1, w_glu_bf, w_out_bf):
    x_spec = pl.BlockSpec((N_SEG, ROW_TAU, D_MODEL), lambda i: (0, i, 0))
    return pl.pallas_call(
        _mix_kernel,
        grid=(N_ROWBLK,),
        in_specs=[
            x_spec,
            pl.BlockSpec((SSM_LB, ROWS, LANES), lambda i: (0, i, 0)),
            pl.BlockSpec((N_SEG, N_HEADS, ROW_TAU, HEAD_DIM), lambda i: (0, 0, i, 0)),
            pl.BlockSpec((BATCH, 1, D_MODEL), lambda i: (0, 0, 0)),
            pl.BlockSpec((SSM_WIDTH, SSM_WIDTH), lambda i: (0, 0)),
            pl.BlockSpec((D_MODEL, D_MODEL), lambda i: (0, 0)),
        ],
        out_specs=x_spec,
        out_shape=jax.ShapeDtypeStruct((N_SEG, SEG_LEN, D_MODEL), F32),
        scratch_shapes=[pltpu.VMEM((ROWS, D_MODEL), BF16)],
        compiler_params=pltpu.CompilerParams(
            dimension_semantics=("parallel",), vmem_limit_bytes=VMEM_LIMIT),
        name="mix",
    )(x3, y_il, att, gate1, w_glu_bf, w_out_bf)


def _ffn_kernel(x_ref, sh_ref, sc_ref, g_ref, nw_ref, wg_ref, wu_ref, wd_ref, o_ref, h_scr):
    j = pl.program_id(1)

    @pl.when(j == 0)
    def _():
        _modulated_norm(x_ref[...], nw_ref[...], sh_ref, sc_ref, h_scr,
                        ((slice(None), 0),))
        o_ref[...] = jnp.zeros_like(o_ref)

    h = h_scr[...]
    g = jnp.dot(h, wg_ref[...], preferred_element_type=F32)
    u = jnp.dot(h, wu_ref[...], preferred_element_type=F32)
    a = (g * _sigmoid(g) * u).astype(BF16)
    o_ref[...] += jnp.dot(a, wd_ref[...], preferred_element_type=F32)

    @pl.when(j == FFN_HIDDEN // FFN_TH - 1)
    def _():
        o_ref[...] = x_ref[...] + g_ref[0] * o_ref[...]


def _ffn(x1, shift2, scale2, gate2, norm2_w, wg_bf, wu_bf, wd_bf):
    tm, th = FFN_TM, FFN_TH
    per_batch = SEQ // tm
    mod_spec = pl.BlockSpec((1, 1, D_MODEL), lambda i, j: (i // per_batch, 0, 0))
    return pl.pallas_call(
        _ffn_kernel,
        grid=(TOKENS // tm, FFN_HIDDEN // th),
        in_specs=[
            pl.BlockSpec((tm, D_MODEL), lambda i, j: (i, 0)),
            mod_spec, mod_spec, mod_spec,
            pl.BlockSpec((1, D_MODEL), lambda i, j: (0, 0)),
            pl.BlockSpec((D_MODEL, th), lambda i, j: (0, j)),
            pl.BlockSpec((D_MODEL, th), lambda i, j: (0, j)),
            pl.BlockSpec((th, D_MODEL), lambda i, j: (j, 0)),
        ],
        out_specs=pl.BlockSpec((tm, D_MODEL), lambda i, j: (i, 0)),
        out_shape=jax.ShapeDtypeStruct((TOKENS, D_MODEL), F32),
        scratch_shapes=[pltpu.VMEM((tm, D_MODEL), BF16)],
        compiler_params=pltpu.CompilerParams(
            dimension_semantics=("parallel", "arbitrary"), vmem_limit_bytes=VMEM_LIMIT),
        name="ffn",
    )(x1, shift2, scale2, gate2, norm2_w, wg_bf, wu_bf, wd_bf)


def kernel(x, c, w_ada, b_ada, norm1_w, w_in, ssm_a_re, ssm_a_im, ssm_log_dt, ssm_b_re, ssm_b_im,
           ssm_c_re, ssm_c_im, ssm_d, ssm_w_glu, q_norm_w, k_norm_w, w_out, norm2_w,
           w_ffn_gate, w_ffn_up, w_ffn_down):
    x3 = x.reshape(N_SEG, SEG_LEN, D_MODEL)
    for i in range(w_ada.shape[0]):
        mod = _ada(c, w_ada[i], b_ada[i])
        shift1, scale1, gate1, shift2, scale2, gate2 = [
            m.reshape(BATCH, 1, D_MODEL) for m in jnp.split(mod, N_MOD, axis=-1)]

        u_il, q, k, v = _inproj(x3, shift1, scale1, norm1_w[i].reshape(1, D_MODEL),
                                w_in[i].astype(BF16),
                                q_norm_w[i].reshape(1, HEAD_DIM), k_norm_w[i].reshape(1, HEAD_DIM))

        a_all, b_exp, c_exp = _ssm_prep(ssm_a_re[i], ssm_a_im[i], ssm_log_dt[i], ssm_b_re[i],
                                        ssm_b_im[i], ssm_c_re[i], ssm_c_im[i])
        y_il = _ssm(u_il, a_all, b_exp, c_exp, ssm_d[i].reshape(1, SSM_WIDTH))
        att = _attention(q, k, v)

        x3 = _mix(x3, y_il, att, gate1, ssm_w_glu[i].astype(BF16), w_out[i].astype(BF16))
        x2 = _ffn(x3.reshape(TOKENS, D_MODEL), shift2, scale2, gate2,
                  norm2_w[i].reshape(1, D_MODEL), w_ffn_gate[i].astype(BF16),
                  w_ffn_up[i].astype(BF16), w_ffn_down[i].astype(BF16))
        x3 = x2.reshape(N_SEG, SEG_LEN, D_MODEL)
    return x3.reshape(BATCH, SEQ, D_MODEL)
```

```python
import numpy as np
import jax
import jax.numpy as jnp
from jax import lax
from jax.experimental import pallas as pl
from jax.experimental.pallas import tpu as pltpu

F32 = jnp.float32
BF16 = jnp.bfloat16

D_MODEL = 2048
BATCH = 2
SEQ = 8192
TOKENS = BATCH * SEQ
SSM_WIDTH = 1024
SSM_GROUP = 16
SSM_GROUPS = 64
SSM_STATE = 64
ATTN_WIDTH = 1024
HEAD_DIM = 128
N_HEADS = 8
BRANCH_DILATIONS = (16, 4, 1)
BRANCH_BLOCK = 128
FFN_HIDDEN = 5632
N_MOD = 6
NORM_EPS = 1e-6
MASK_VALUE = -1e30
LANES = 128

N_SEG = 8
SEG_LEN = TOKENS // N_SEG
SEG_PER_BATCH = SEQ // SEG_LEN
ROWS = 512
BLK_PER_SEG = SEG_LEN // ROWS
BLK_PER_BATCH = SEQ // ROWS

SUBLANES = 8
SC_TAU = 32
SC_LEN = SUBLANES * SC_TAU
SC_PER_ROWBLK = ROWS // SC_LEN
SSM_ROWS = 1024
SSM_SC = SSM_ROWS // SC_LEN
SSM_LB = SSM_WIDTH // LANES
SSM_COLSETS = 4
COLSET_CH = SSM_WIDTH // SSM_COLSETS
COLSET_ST = 16 * SSM_STATE

ATT_Q = SEG_LEN // 4

ADA_TN = 1024
FFN_TM = 512
FFN_TH = 512

VMEM_LIMIT = 56 * 1024 * 1024


def _sigmoid(x):
    return 1.0 / (1.0 + jnp.exp(-x))


def _modulated_norm(x, nw, shift, scale):
    ms = jnp.mean(x * x, axis=-1, keepdims=True)
    y = x * lax.rsqrt(ms + NORM_EPS) * nw
    return (y * (1.0 + scale) + shift).astype(BF16)


def _interleaved_pieces():
    return [(slice(sc * SC_LEN + s * SC_TAU, sc * SC_LEN + (s + 1) * SC_TAU), sc * SC_LEN + s)
            for sc in range(SC_PER_ROWBLK) for s in range(SUBLANES)]


def _ada_kernel(ct_ref, w_ref, b_ref, o_ref, cond_scr):
    c = ct_ref[...]
    cond_scr[...] = c * _sigmoid(c)

    def body(kc, acc):
        row = pl.multiple_of(kc * 8, 8)
        wk = w_ref[pl.ds(row, 8), :]
        ck = cond_scr[pl.ds(row, 8), :]
        return tuple(acc[b] + ck[:, b:b + 1] * wk for b in range(BATCH))

    zero = jnp.zeros((8, ADA_TN), F32)
    acc = lax.fori_loop(0, D_MODEL // 8, body, (zero,) * BATCH, unroll=8)
    rows = [jnp.sum(a, axis=0, keepdims=True) for a in acc]
    o_ref[...] = jnp.concatenate(rows, axis=0) + b_ref[...]


def _ada(c, w_ada, b_ada):
    n = N_MOD * D_MODEL
    return pl.pallas_call(
        _ada_kernel,
        grid=(n // ADA_TN,),
        in_specs=[
            pl.BlockSpec((D_MODEL, BATCH), lambda j: (0, 0)),
            pl.BlockSpec((D_MODEL, ADA_TN), lambda j: (0, j)),
            pl.BlockSpec((1, ADA_TN), lambda j: (0, j)),
        ],
        out_specs=pl.BlockSpec((BATCH, ADA_TN), lambda j: (0, j)),
        out_shape=jax.ShapeDtypeStruct((BATCH, n), F32),
        scratch_shapes=[pltpu.VMEM((D_MODEL, BATCH), F32)],
        compiler_params=pltpu.CompilerParams(
            dimension_semantics=("arbitrary",), vmem_limit_bytes=VMEM_LIMIT),
        name="ada",
    )(c.T, w_ada, b_ada.reshape(1, n))


def _head_rms(blk, w):
    ms = jnp.mean(blk * blk, axis=-1, keepdims=True)
    return blk * lax.rsqrt(ms + NORM_EPS) * w


def _inproj_kernel(x_ref, sh_ref, sc_ref, nw_ref, w_ref, qw_ref, kw_ref,
                   u_ref, q_ref, k_ref, v_ref, h_scr):
    h_scr[...] = _modulated_norm(x_ref[...], nw_ref[...], sh_ref[0], sc_ref[0])
    h = h_scr[...]

    def proj(n):
        return jnp.dot(h, w_ref[:, n * 1024:(n + 1) * 1024], preferred_element_type=F32)

    def per_head(res, out_ref, fn):
        for hd in range(N_HEADS):
            out_ref[0, hd] = fn(res[:, hd * HEAD_DIM:(hd + 1) * HEAD_DIM]).astype(BF16)

    res = proj(0)
    for lb in range(SSM_LB):
        for rows, start in _interleaved_pieces():
            u_ref[lb, pl.ds(start, SC_TAU, stride=SUBLANES), :] = (
                res[rows, lb * LANES:(lb + 1) * LANES])
    per_head(proj(1), q_ref, lambda blk: _head_rms(blk, qw_ref[...]) * (HEAD_DIM ** -0.5))
    per_head(proj(2), k_ref, lambda blk: _head_rms(blk, kw_ref[...]))
    per_head(proj(3), v_ref, lambda blk: blk)


def _inproj(x2, shift1, scale1, norm1_w, w_in_bf, q_norm_w, k_norm_w):
    qkv_shape = jax.ShapeDtypeStruct((N_SEG, N_HEADS, SEG_LEN, HEAD_DIM), BF16)
    qkv_spec = pl.BlockSpec((1, N_HEADS, ROWS, HEAD_DIM),
                            lambda i: (i // BLK_PER_SEG, 0, i % BLK_PER_SEG, 0))
    mod_spec = pl.BlockSpec((1, 1, D_MODEL), lambda i: (i // BLK_PER_BATCH, 0, 0))
    return pl.pallas_call(
        _inproj_kernel,
        grid=(TOKENS // ROWS,),
        in_specs=[
            pl.BlockSpec((ROWS, D_MODEL), lambda i: (i, 0)),
            mod_spec, mod_spec,
            pl.BlockSpec((1, D_MODEL), lambda i: (0, 0)),
            pl.BlockSpec((D_MODEL, 4 * 1024), lambda i: (0, 0), pipeline_mode=pl.Buffered(1)),
            pl.BlockSpec((1, HEAD_DIM), lambda i: (0, 0)),
            pl.BlockSpec((1, HEAD_DIM), lambda i: (0, 0)),
        ],
        out_specs=[
            pl.BlockSpec((SSM_LB, ROWS, LANES), lambda i: (0, i, 0)),
            qkv_spec, qkv_spec, qkv_spec,
        ],
        out_shape=[
            jax.ShapeDtypeStruct((SSM_LB, TOKENS, LANES), F32),
            qkv_shape, qkv_shape, qkv_shape,
        ],
        scratch_shapes=[pltpu.VMEM((ROWS, D_MODEL), BF16)],
        compiler_params=pltpu.CompilerParams(
            dimension_semantics=("parallel",), vmem_limit_bytes=VMEM_LIMIT),
        name="inproj",
    )(x2, shift1, scale1, norm1_w, w_in_bf, q_norm_w, k_norm_w)


def _ssm_prep_kernel(are_ref, aim_ref, ldt_ref, bre_ref, bim_ref, cim_ref,
                     abr_ref, abi_ref, bbr_ref, bbi_ref, ncim_ref):
    a_re = are_ref[...]
    a_im = aim_ref[...]
    dt = jnp.exp(ldt_ref[...])
    mag = jnp.exp(a_re * dt)
    abar_re = mag * jnp.cos(a_im * dt)
    abar_im = mag * jnp.sin(a_im * dt)
    num_re = abar_re - 1.0
    num_im = abar_im
    den = a_re * a_re + a_im * a_im
    z_re = (num_re * a_re + num_im * a_im) / den
    z_im = (num_im * a_re - num_re * a_im) / den
    b_re = bre_ref[...]
    b_im = bim_ref[...]
    abr_ref[...] = abar_re
    abi_ref[...] = abar_im
    bbr_ref[...] = z_re * b_re - z_im * b_im
    bbi_ref[...] = z_re * b_im + z_im * b_re
    ncim_ref[...] = -cim_ref[...]


def _ssm_prep(a_re, a_im, log_dt, b_re, b_im, c_re, c_im):
    G, N, C = SSM_GROUPS, SSM_STATE, SSM_GROUP
    tile_n = lambda a: jnp.tile(a, (1, C))
    cn = lambda b: b.transpose(0, 2, 1).reshape(G, C * N)
    shp = jax.ShapeDtypeStruct((G, C * N), F32)
    abr, abi, bbr, bbi, ncim = pl.pallas_call(
        _ssm_prep_kernel,
        out_shape=[shp] * 5,
        name="ssm_prep",
    )(tile_n(a_re), tile_n(a_im), log_dt.reshape(G, 1), cn(b_re), cn(b_im),
      c_im.reshape(G, C * N))

    eye = jnp.eye(16, dtype=F32)
    a_all = jnp.stack([abr[:, :N], abi[:, :N]])
    a_all = a_all.reshape(2, SSM_COLSETS, 16 * N).transpose(1, 0, 2)
    a_all = a_all.reshape(SSM_COLSETS, 1, 2 * COLSET_ST)
    bb = jnp.stack([bbr, bbi]).reshape(2, SSM_COLSETS, 16, C, N)
    b_exp = jnp.einsum('psgcn,gh->sgcphn', bb, eye)
    b_exp = b_exp.reshape(SSM_COLSETS, COLSET_CH, 2 * COLSET_ST).astype(BF16)
    cc = jnp.stack([c_re.reshape(G, C * N), ncim]).reshape(2, SSM_COLSETS, 16, C, N)
    c_exp = jnp.einsum('psgcn,gh->sphngc', cc, eye)
    c_exp = c_exp.reshape(SSM_COLSETS, 2 * COLSET_ST, COLSET_CH).astype(BF16)
    return a_all, b_exp, c_exp


def _cmul(ar, ai, br, bi):
    return ar * br - ai * bi, ar * bi + ai * br


def _ssm_kernel(u_ref, b_ref, c_ref, a_ref, d_ref, y_ref, bu_scr, a_scr, carry_scr):
    ch = pl.program_id(2)
    S = COLSET_ST

    @pl.when(ch == 0)
    def _():
        carry_scr[...] = jnp.zeros_like(carry_scr)

    a_scr[...] = jnp.broadcast_to(a_ref[0], (SUBLANES, 2 * S))

    def abar():
        return a_scr[:, :S], a_scr[:, S:]

    pw = [abar()]
    for _ in range(SC_TAU.bit_length() + 1):
        pw.append(_cmul(*pw[-1], *pw[-1]))
    p1, p2, p4 = pw[-3:]
    sub = lax.broadcasted_iota(jnp.int32, (SUBLANES, S), 0)

    def shifted(v, k):
        return jnp.where(sub >= k, pltpu.roll(v, k, 0), 0.0)

    def tile(base, t):
        return slice(base + t * SUBLANES, base + (t + 1) * SUBLANES)

    def scan(base, hr, hi, store):
        for t in range(1, SC_TAU):
            rows = tile(base, t)
            ar, ai = abar()
            hr, hi = (ar * hr - ai * hi + bu_scr[rows, :S],
                      ar * hi + ai * hr + bu_scr[rows, S:])
            if store:
                bu_scr[rows, :S] = hr
                bu_scr[rows, S:] = hi
        return hr, hi

    u = jnp.concatenate([u_ref[0], u_ref[1]], axis=1)
    u_bf = u.astype(BF16)

    def project_in(k):
        rows = slice(k * SC_LEN, (k + 1) * SC_LEN)
        bu_scr[rows, :] = jnp.dot(u_bf[rows], b_ref[0], preferred_element_type=F32)

    def project_out(k):
        rows = slice(k * SC_LEN, (k + 1) * SC_LEN)
        y = jnp.dot(bu_scr[rows, :].astype(BF16), c_ref[0], preferred_element_type=F32)
        y = jax.nn.gelu(y + d_ref[...] * u[rows])
        y_ref[0, rows, :] = y[:, :LANES]
        y_ref[1, rows, :] = y[:, LANES:]

    cr, ci = carry_scr[:, :S], carry_scr[:, S:]
    project_in(0)
    for k in range(SSM_SC):
        if k + 1 < SSM_SC:
            project_in(k + 1)
        base = k * SC_LEN
        first = tile(base, 0)
        er, ei = scan(base, bu_scr[first, :S], bu_scr[first, S:], False)
        xr = jnp.where(sub == 0, cr, pltpu.roll(er, 1, 0))
        xi = jnp.where(sub == 0, ci, pltpu.roll(ei, 1, 0))
        for (pr, pi), sh in ((p1, 1), (p2, 2), (p4, 4)):
            tr, ti = _cmul(pr, pi, shifted(xr, sh), shifted(xi, sh))
            xr, xi = xr + tr, xi + ti
        tr, ti = _cmul(*p1, xr, xi)
        cr, ci = pltpu.roll(tr + er, 1, 0), pltpu.roll(ti + ei, 1, 0)
        tr, ti = _cmul(*abar(), xr, xi)
        hr, hi = bu_scr[first, :S] + tr, bu_scr[first, S:] + ti
        bu_scr[first, :S] = hr
        bu_scr[first, S:] = hi
        scan(base, hr, hi, True)
        project_out(k)
    carry_scr[:, :S] = cr
    carry_scr[:, S:] = ci


def _ssm(u_il, a_all, b_exp, c_exp, d_flat):
    per_batch = SEQ // SSM_ROWS
    u_spec = pl.BlockSpec((2, SSM_ROWS, LANES), lambda s, b, c: (s, b * per_batch + c, 0))
    state = pltpu.VMEM((SUBLANES, 2 * COLSET_ST), F32)
    return pl.pallas_call(
        _ssm_kernel,
        grid=(SSM_COLSETS, BATCH, per_batch),
        in_specs=[
            u_spec,
            pl.BlockSpec((1, COLSET_CH, 2 * COLSET_ST), lambda s, b, c: (s, 0, 0)),
            pl.BlockSpec((1, 2 * COLSET_ST, COLSET_CH), lambda s, b, c: (s, 0, 0)),
            pl.BlockSpec((1, 1, 2 * COLSET_ST), lambda s, b, c: (s, 0, 0)),
            pl.BlockSpec((1, COLSET_CH), lambda s, b, c: (0, s)),
        ],
        out_specs=u_spec,
        out_shape=jax.ShapeDtypeStruct((SSM_LB, TOKENS, LANES), F32),
        scratch_shapes=[pltpu.VMEM((SSM_ROWS, 2 * COLSET_ST), F32), state, state],
        compiler_params=pltpu.CompilerParams(
            dimension_semantics=("parallel", "parallel", "arbitrary"),
            vmem_limit_bytes=VMEM_LIMIT),
        name="ssm",
    )(u_il, b_exp, c_exp, a_all, d_flat)


def _branch_distances():
    out = []
    for dil in BRANCH_DILATIONS:
        per = 32 if dil == 1 else BRANCH_BLOCK
        idx = np.arange(BRANCH_BLOCK)
        pos = (BRANCH_BLOCK // per) * (idx % per) + idx // per
        dist = np.concatenate([pos[:, None] - pos[None, :] + BRANCH_BLOCK,
                               pos[:, None] - pos[None, :]], axis=1)
        ok = (dist >= 0) & (dist <= BRANCH_BLOCK)
        out.append(np.where(ok, dil * dist, -1).astype(np.float32))
    return np.stack(out)


def _attn_tile(qt, kt, vt, bm):
    s = lax.dot_general(qt, kt, (((1,), (1,)), ((), ())), preferred_element_type=F32)
    s = jnp.where(bm > 0.5 * MASK_VALUE, s + bm, MASK_VALUE)
    m = jnp.max(s, axis=-1, keepdims=True)
    p = jnp.exp(s - m)
    l = jnp.sum(p, axis=-1, keepdims=True)
    o = jnp.dot(p.astype(BF16), vt, preferred_element_type=F32)
    return m, l, o


def _attn_kernel(q_ref, k_ref, v_ref, dd_ref, o_ref,
                 nat_scr, q4, k4, v4, m4, l4, o4, bias_scr):
    h = pl.program_id(1)
    i = pl.program_id(2)
    Q = ATT_Q
    E = HEAD_DIM
    BLK = BRANCH_BLOCK
    cur = pl.multiple_of((i % 2) * Q, Q)
    prv = pl.multiple_of(Q - (i % 2) * Q, Q)

    def to_slabs(src_ref, dst, base):
        nat_scr[...] = src_ref[0, 0].astype(F32)
        for r in range(4):
            dst[r, pl.ds(base, Q), :] = nat_scr[pl.ds(r, Q, stride=4), :]

    to_slabs(q_ref, q4, 0)
    to_slabs(k_ref, k4, cur)
    to_slabs(v_ref, v4, cur)

    @pl.when(i == 0)
    def _():
        for r in range(4):
            k4[r, pl.ds(prv, Q), :] = jnp.zeros((Q, E), F32)
            v4[r, pl.ds(prv, Q), :] = jnp.zeros((Q, E), F32)

    hv = jnp.full((BLK, 2 * BLK), h, jnp.int32).astype(F32)
    slope = jnp.exp2(-8.0 * (hv + 1.0) / N_HEADS)
    is_prev = lax.broadcasted_iota(jnp.int32, (BLK, 2 * BLK), 1) < BLK
    for br in range(3):
        dd = dd_ref[br]
        bm = jnp.where(dd >= 0.0, (-slope) * dd, MASK_VALUE)
        bias_scr[2 * br] = bm
        bias_scr[2 * br + 1] = jnp.where(is_prev, MASK_VALUE, bm)

    def load(scr, chunks):
        return jnp.concatenate([scr[c] for c in chunks], axis=0)

    def merge(chunks, nrows, m_t, l_t, o_t, first, last):
        for n, c in enumerate(chunks):
            sl = slice(n * nrows, (n + 1) * nrows)
            m_c = jnp.broadcast_to(m_t[sl], (nrows, E))
            l_c = jnp.broadcast_to(l_t[sl], (nrows, E))
            o_c = o_t[sl]
            if first:
                m4[c] = m_c
                l4[c] = l_c
                o4[c] = o_c
                continue
            m_old = m4[c]
            m_new = jnp.maximum(m_old, m_c)
            alpha = jnp.exp(m_old - m_new)
            beta = jnp.exp(m_c - m_new)
            l_new = alpha * l4[c] + beta * l_c
            o_new = alpha * o4[c] + beta * o_c
            if last:
                o4[c] = o_new / l_new
            else:
                m4[c] = m_new
                l4[c] = l_new
                o4[c] = o_new

    def run_tile(br, q_chunks, kp_chunks, kc_chunks, nrows, seq_start):
        qt = load(q4, q_chunks).astype(BF16)
        kt = jnp.concatenate([load(k4, kp_chunks), load(k4, kc_chunks)], axis=0).astype(BF16)
        vt = jnp.concatenate([load(v4, kp_chunks), load(v4, kc_chunks)], axis=0).astype(BF16)
        if seq_start:
            bm = bias_scr[2 * br + (i == 0).astype(jnp.int32)]
        else:
            bm = bias_scr[2 * br]
        m_t, l_t, o_t = _attn_tile(qt, kt, vt, bm)
        merge(q_chunks, nrows, m_t, l_t, o_t, br == 0, br == 2)

    for r in range(4):
        for r2 in range(4):
            run_tile(0,
                     [(r, pl.ds(r2, BLK, stride=4), slice(None))],
                     [(r, pl.ds(prv + r2, BLK, stride=4), slice(None))],
                     [(r, pl.ds(cur + r2, BLK, stride=4), slice(None))],
                     BLK, True)

    for r in range(4):
        for b in range(4):
            kp = prv + (Q - BLK) if b == 0 else cur + (b - 1) * BLK
            run_tile(1,
                     [(r, pl.ds(b * BLK, BLK), slice(None))],
                     [(r, pl.ds(pl.multiple_of(kp, BLK), BLK), slice(None))],
                     [(r, pl.ds(pl.multiple_of(cur + b * BLK, BLK), BLK), slice(None))],
                     BLK, b == 0)

    for b in range(SEG_LEN // BLK):
        kp = prv + (Q - 32) if b == 0 else cur + (b - 1) * 32
        run_tile(2,
                 [(r, pl.ds(b * 32, 32), slice(None)) for r in range(4)],
                 [(r, pl.ds(pl.multiple_of(kp, 32), 32), slice(None)) for r in range(4)],
                 [(r, pl.ds(pl.multiple_of(cur + b * 32, 32), 32), slice(None))
                  for r in range(4)],
                 32, b == 0)

    for r in range(4):
        nat_scr[pl.ds(r, Q, stride=4), :] = o4[r]
    o_ref[0, 0] = nat_scr[...].astype(BF16)


def _attention(q, k, v):
    blk = pl.BlockSpec((1, 1, SEG_LEN, HEAD_DIM),
                       lambda b, h, i: (b * SEG_PER_BATCH + i, h, 0, 0))
    dd = jnp.asarray(_branch_distances())
    slab = pltpu.VMEM((4, ATT_Q, HEAD_DIM), F32)
    slab2 = pltpu.VMEM((4, 2 * ATT_Q, HEAD_DIM), F32)
    return pl.pallas_call(
        _attn_kernel,
        grid=(BATCH, N_HEADS, SEG_PER_BATCH),
        in_specs=[blk, blk, blk,
                  pl.BlockSpec((3, BRANCH_BLOCK, 2 * BRANCH_BLOCK), lambda b, h, i: (0, 0, 0))],
        out_specs=blk,
        out_shape=jax.ShapeDtypeStruct((N_SEG, N_HEADS, SEG_LEN, HEAD_DIM), BF16),
        scratch_shapes=[
            pltpu.VMEM((SEG_LEN, HEAD_DIM), F32),
            slab, slab2, slab2,
            slab, slab, slab,
            pltpu.VMEM((6, BRANCH_BLOCK, 2 * BRANCH_BLOCK), F32),
        ],
        compiler_params=pltpu.CompilerParams(
            dimension_semantics=("parallel", "parallel", "arbitrary"),
            vmem_limit_bytes=VMEM_LIMIT),
        name="attn",
    )(q, k, v, dd)


def _mix_kernel(x_ref, y_ref, att_ref, g_ref, wglu_ref, wout_ref, o_ref, cat_scr):
    y = jnp.concatenate(
        [jnp.concatenate([y_ref[lb, pl.ds(start, SC_TAU, stride=SUBLANES), :]
                          for _, start in _interleaved_pieces()], axis=0)
         for lb in range(SSM_LB)], axis=1)
    gl = jnp.dot(y.astype(BF16), wglu_ref[...], preferred_element_type=F32)
    cat_scr[:, :SSM_WIDTH] = (y * _sigmoid(gl)).astype(BF16)
    for h in range(N_HEADS):
        c0 = SSM_WIDTH + h * HEAD_DIM
        cat_scr[:, c0:c0 + HEAD_DIM] = att_ref[0, h]
    mixed = jnp.dot(cat_scr[...], wout_ref[...], preferred_element_type=F32)
    o_ref[...] = x_ref[...] + g_ref[0] * mixed


def _mix(x2, y_il, att, gate1, w_glu_bf, w_out_bf):
    x_spec = pl.BlockSpec((ROWS, D_MODEL), lambda i: (i, 0))
    return pl.pallas_call(
        _mix_kernel,
        grid=(TOKENS // ROWS,),
        in_specs=[
            x_spec,
            pl.BlockSpec((SSM_LB, ROWS, LANES), lambda i: (0, i, 0)),
            pl.BlockSpec((1, N_HEADS, ROWS, HEAD_DIM),
                         lambda i: (i // BLK_PER_SEG, 0, i % BLK_PER_SEG, 0)),
            pl.BlockSpec((1, 1, D_MODEL), lambda i: (i // BLK_PER_BATCH, 0, 0)),
            pl.BlockSpec((SSM_WIDTH, SSM_WIDTH), lambda i: (0, 0)),
            pl.BlockSpec((D_MODEL, D_MODEL), lambda i: (0, 0)),
        ],
        out_specs=x_spec,
        out_shape=jax.ShapeDtypeStruct((TOKENS, D_MODEL), F32),
        scratch_shapes=[pltpu.VMEM((ROWS, D_MODEL), BF16)],
        compiler_params=pltpu.CompilerParams(
            dimension_semantics=("parallel",), vmem_limit_bytes=VMEM_LIMIT),
        name="mix",
    )(x2, y_il, att, gate1, w_glu_bf, w_out_bf)


def _ffn_kernel(x_ref, sh_ref, sc_ref, g_ref, nw_ref, wg_ref, wu_ref, wd_ref, o_ref, h_scr):
    j = pl.program_id(1)

    @pl.when(j == 0)
    def _():
        h_scr[...] = _modulated_norm(x_ref[...], nw_ref[...], sh_ref[0], sc_ref[0])
        o_ref[...] = jnp.zeros_like(o_ref)

    h = h_scr[...]
    g = jnp.dot(h, wg_ref[...], preferred_element_type=F32)
    u = jnp.dot(h, wu_ref[...], preferred_element_type=F32)
    a = (g * _sigmoid(g) * u).astype(BF16)
    o_ref[...] += jnp.dot(a, wd_ref[...], preferred_element_type=F32)

    @pl.when(j == FFN_HIDDEN // FFN_TH - 1)
    def _():
        o_ref[...] = x_ref[...] + g_ref[0] * o_ref[...]


def _ffn(x1, shift2, scale2, gate2, norm2_w, wg_bf, wu_bf, wd_bf):
    tm, th = FFN_TM, FFN_TH
    per_batch = SEQ // tm
    mod_spec = pl.BlockSpec((1, 1, D_MODEL), lambda i, j: (i // per_batch, 0, 0))
    return pl.pallas_call(
        _ffn_kernel,
        grid=(TOKENS // tm, FFN_HIDDEN // th),
        in_specs=[
            pl.BlockSpec((tm, D_MODEL), lambda i, j: (i, 0)),
            mod_spec, mod_spec, mod_spec,
            pl.BlockSpec((1, D_MODEL), lambda i, j: (0, 0)),
            pl.BlockSpec((D_MODEL, th), lambda i, j: (0, j)),
            pl.BlockSpec((D_MODEL, th), lambda i, j: (0, j)),
            pl.BlockSpec((th, D_MODEL), lambda i, j: (j, 0)),
        ],
        out_specs=pl.BlockSpec((tm, D_MODEL), lambda i, j: (i, 0)),
        out_shape=jax.ShapeDtypeStruct((TOKENS, D_MODEL), F32),
        scratch_shapes=[pltpu.VMEM((tm, D_MODEL), BF16)],
        compiler_params=pltpu.CompilerParams(
            dimension_semantics=("parallel", "arbitrary"), vmem_limit_bytes=VMEM_LIMIT),
        name="ffn",
    )(x1, shift2, scale2, gate2, norm2_w, wg_bf, wu_bf, wd_bf)


def kernel(x, c, w_ada, b_ada, norm1_w, w_in, ssm_a_re, ssm_a_im, ssm_log_dt, ssm_b_re, ssm_b_im,
           ssm_c_re, ssm_c_im, ssm_d, ssm_w_glu, q_norm_w, k_norm_w, w_out, norm2_w,
           w_ffn_gate, w_ffn_up, w_ffn_down):
    x2 = x.reshape(TOKENS, D_MODEL)
    for i in range(w_ada.shape[0]):
        mod = _ada(c, w_ada[i], b_ada[i])
        shift1, scale1, gate1, shift2, scale2, gate2 = [
            m.reshape(BATCH, 1, D_MODEL) for m in jnp.split(mod, N_MOD, axis=-1)]

        u_il, q, k, v = _inproj(x2, shift1, scale1, norm1_w[i].reshape(1, D_MODEL),
                                w_in[i].astype(BF16),
                                q_norm_w[i].reshape(1, HEAD_DIM), k_norm_w[i].reshape(1, HEAD_DIM))

        a_all, b_exp, c_exp = _ssm_prep(ssm_a_re[i], ssm_a_im[i], ssm_log_dt[i], ssm_b_re[i],
                                        ssm_b_im[i], ssm_c_re[i], ssm_c_im[i])
        y_il = _ssm(u_il, a_all, b_exp, c_exp, ssm_d[i].reshape(1, SSM_WIDTH))
        att = _attention(q, k, v)

        x2 = _mix(x2, y_il, att, gate1, ssm_w_glu[i].astype(BF16), w_out[i].astype(BF16))
        x2 = _ffn(x2, shift2, scale2, gate2,
                  norm2_w[i].reshape(1, D_MODEL), w_ffn_gate[i].astype(BF16),
                  w_ffn_up[i].astype(BF16), w_ffn_down[i].astype(BF16))
    return x2.reshape(BATCH, SEQ, D_MODEL)
```

```python
import numpy as np
import jax
import jax.numpy as jnp
from jax import lax
from jax.experimental import pallas as pl
from jax.experimental.pallas import tpu as pltpu

F32 = jnp.float32
BF16 = jnp.bfloat16

D_MODEL = 2048
BATCH = 2
SEQ = 8192
TOKENS = BATCH * SEQ
SSM_WIDTH = 1024
SSM_GROUP = 16
SSM_GROUPS = 64
SSM_STATE = 64
ATTN_WIDTH = 1024
HEAD_DIM = 128
N_HEADS = 8
BRANCH_DILATIONS = (16, 4, 1)
BRANCH_BLOCK = 128
FFN_HIDDEN = 5632
N_MOD = 6
NORM_EPS = 1e-6
MASK_VALUE = -1e30
LANES = 128

N_SEG = 8
SEG_LEN = TOKENS // N_SEG
SEG_PER_BATCH = SEQ // SEG_LEN
ROWS = 512
BLK_PER_SEG = SEG_LEN // ROWS
BLK_PER_BATCH = SEQ // ROWS

SUBLANES = 8
SC_TAU = 32
SC_LEN = SUBLANES * SC_TAU
SC_PER_ROWBLK = ROWS // SC_LEN
SSM_ROWS = 1024
SSM_SC = SSM_ROWS // SC_LEN
SSM_LB = SSM_WIDTH // LANES
SSM_COLSETS = 4
COLSET_CH = SSM_WIDTH // SSM_COLSETS
COLSET_ST = 16 * SSM_STATE

ATT_Q = SEG_LEN // 4

ADA_TN = 1024
FFN_TM = 1024
FFN_TH = 512
FFN_PROLOGUE_SPLIT = 4

VMEM_LIMIT = 56 * 1024 * 1024


def _sigmoid(x):
    return 1.0 / (1.0 + jnp.exp(-x))


def _modulated_norm(x, nw, shift, scale):
    ms = jnp.mean(x * x, axis=-1, keepdims=True)
    y = x * lax.rsqrt(ms + NORM_EPS) * nw
    return (y * (1.0 + scale) + shift).astype(BF16)


def _interleaved_pieces():
    return [(slice(sc * SC_LEN + s * SC_TAU, sc * SC_LEN + (s + 1) * SC_TAU), sc * SC_LEN + s)
            for sc in range(SC_PER_ROWBLK) for s in range(SUBLANES)]


def _ada_kernel(ct_ref, w_ref, b_ref, o_ref, cond_scr):
    @pl.when(pl.program_id(0) == 0)
    def _():
        c = ct_ref[...]
        cond = c * _sigmoid(c)
        for b in range(BATCH):
            cond_scr[b] = jnp.broadcast_to(cond[:, b:b + 1], (D_MODEL, LANES))

    nlb = ADA_TN // LANES

    def body(kc, acc):
        row = pl.multiple_of(kc * 8, 8)
        out = []
        for b in range(BATCH):
            ck = cond_scr[b, pl.ds(row, 8), :]
            for lb in range(nlb):
                wk = w_ref[pl.ds(row, 8), lb * LANES:(lb + 1) * LANES]
                out.append(acc[b * nlb + lb] + ck * wk)
        return tuple(out)

    zero = jnp.zeros((8, LANES), F32)
    acc = lax.fori_loop(0, D_MODEL // 8, body, (zero,) * (BATCH * nlb), unroll=8)
    rows = [jnp.concatenate([jnp.sum(acc[b * nlb + lb], axis=0, keepdims=True)
                             for lb in range(nlb)], axis=1) for b in range(BATCH)]
    o_ref[...] = jnp.concatenate(rows, axis=0) + b_ref[...]


def _ada(c, w_ada, b_ada):
    n = N_MOD * D_MODEL
    return pl.pallas_call(
        _ada_kernel,
        grid=(n // ADA_TN,),
        in_specs=[
            pl.BlockSpec((D_MODEL, BATCH), lambda j: (0, 0)),
            pl.BlockSpec((D_MODEL, ADA_TN), lambda j: (0, j)),
            pl.BlockSpec((1, ADA_TN), lambda j: (0, j)),
        ],
        out_specs=pl.BlockSpec((BATCH, ADA_TN), lambda j: (0, j)),
        out_shape=jax.ShapeDtypeStruct((BATCH, n), F32),
        scratch_shapes=[pltpu.VMEM((BATCH, D_MODEL, LANES), F32)],
        compiler_params=pltpu.CompilerParams(
            dimension_semantics=("arbitrary",), vmem_limit_bytes=VMEM_LIMIT),
        name="ada",
    )(c.T, w_ada, b_ada.reshape(1, n))


def _head_rms(blk, w):
    ms = jnp.mean(blk * blk, axis=-1, keepdims=True)
    return blk * lax.rsqrt(ms + NORM_EPS) * w


def _inproj_kernel(x_ref, sh_ref, sc_ref, nw_ref, w_ref, qw_ref, kw_ref,
                   u_ref, q_ref, k_ref, v_ref, h_scr):
    h_scr[...] = _modulated_norm(x_ref[...], nw_ref[...], sh_ref[0], sc_ref[0])
    h = h_scr[...]

    def proj(n):
        return jnp.dot(h, w_ref[:, n * 1024:(n + 1) * 1024], preferred_element_type=F32)

    def per_head(res, out_ref, fn):
        for hd in range(N_HEADS):
            out_ref[0, hd] = fn(res[:, hd * HEAD_DIM:(hd + 1) * HEAD_DIM]).astype(BF16)

    res = proj(0)
    for lb in range(SSM_LB):
        for rows, start in _interleaved_pieces():
            u_ref[lb, pl.ds(start, SC_TAU, stride=SUBLANES), :] = (
                res[rows, lb * LANES:(lb + 1) * LANES])
    per_head(proj(1), q_ref, lambda blk: _head_rms(blk, qw_ref[...]) * (HEAD_DIM ** -0.5))
    per_head(proj(2), k_ref, lambda blk: _head_rms(blk, kw_ref[...]))
    per_head(proj(3), v_ref, lambda blk: blk)


def _inproj(x2, shift1, scale1, norm1_w, w_in_bf, q_norm_w, k_norm_w):
    qkv_shape = jax.ShapeDtypeStruct((N_SEG, N_HEADS, SEG_LEN, HEAD_DIM), BF16)
    qkv_spec = pl.BlockSpec((1, N_HEADS, ROWS, HEAD_DIM),
                            lambda i: (i // BLK_PER_SEG, 0, i % BLK_PER_SEG, 0))
    mod_spec = pl.BlockSpec((1, 1, D_MODEL), lambda i: (i // BLK_PER_BATCH, 0, 0))
    return pl.pallas_call(
        _inproj_kernel,
        grid=(TOKENS // ROWS,),
        in_specs=[
            pl.BlockSpec((ROWS, D_MODEL), lambda i: (i, 0)),
            mod_spec, mod_spec,
            pl.BlockSpec((1, D_MODEL), lambda i: (0, 0)),
            pl.BlockSpec((D_MODEL, 4 * 1024), lambda i: (0, 0), pipeline_mode=pl.Buffered(1)),
            pl.BlockSpec((1, HEAD_DIM), lambda i: (0, 0)),
            pl.BlockSpec((1, HEAD_DIM), lambda i: (0, 0)),
        ],
        out_specs=[
            pl.BlockSpec((SSM_LB, ROWS, LANES), lambda i: (0, i, 0)),
            qkv_spec, qkv_spec, qkv_spec,
        ],
        out_shape=[
            jax.ShapeDtypeStruct((SSM_LB, TOKENS, LANES), F32),
            qkv_shape, qkv_shape, qkv_shape,
        ],
        scratch_shapes=[pltpu.VMEM((ROWS, D_MODEL), BF16)],
        compiler_params=pltpu.CompilerParams(
            dimension_semantics=("parallel",), vmem_limit_bytes=VMEM_LIMIT),
        name="inproj",
    )(x2, shift1, scale1, norm1_w, w_in_bf, q_norm_w, k_norm_w)


def _ssm_prep_kernel(are_ref, aim_ref, ldt_ref, bre_ref, bim_ref, cim_ref,
                     abr_ref, abi_ref, bbr_ref, bbi_ref, ncim_ref):
    a_re = are_ref[...]
    a_im = aim_ref[...]
    dt = jnp.exp(ldt_ref[...])
    mag = jnp.exp(a_re * dt)
    abar_re = mag * jnp.cos(a_im * dt)
    abar_im = mag * jnp.sin(a_im * dt)
    num_re = abar_re - 1.0
    num_im = abar_im
    den = a_re * a_re + a_im * a_im
    z_re = (num_re * a_re + num_im * a_im) / den
    z_im = (num_im * a_re - num_re * a_im) / den
    b_re = bre_ref[...]
    b_im = bim_ref[...]
    abr_ref[...] = abar_re
    abi_ref[...] = abar_im
    bbr_ref[...] = z_re * b_re - z_im * b_im
    bbi_ref[...] = z_re * b_im + z_im * b_re
    ncim_ref[...] = -cim_ref[...]


def _ssm_prep(a_re, a_im, log_dt, b_re, b_im, c_re, c_im):
    G, N, C = SSM_GROUPS, SSM_STATE, SSM_GROUP
    tile_n = lambda a: jnp.tile(a, (1, C))
    cn = lambda b: b.transpose(0, 2, 1).reshape(G, C * N)
    shp = jax.ShapeDtypeStruct((G, C * N), F32)
    abr, abi, bbr, bbi, ncim = pl.pallas_call(
        _ssm_prep_kernel,
        out_shape=[shp] * 5,
        name="ssm_prep",
    )(tile_n(a_re), tile_n(a_im), log_dt.reshape(G, 1), cn(b_re), cn(b_im),
      c_im.reshape(G, C * N))

    eye = jnp.eye(16, dtype=F32)
    a_all = jnp.stack([abr[:, :N], abi[:, :N]])
    a_all = a_all.reshape(2, SSM_COLSETS, 16 * N).transpose(1, 0, 2)
    a_all = a_all.reshape(SSM_COLSETS, 1, 2 * COLSET_ST)
    bb = jnp.stack([bbr, bbi]).reshape(2, SSM_COLSETS, 16, C, N)
    b_exp = jnp.einsum('psgcn,gh->sgcphn', bb, eye)
    b_exp = b_exp.reshape(SSM_COLSETS, COLSET_CH, 2 * COLSET_ST).astype(BF16)
    cc = jnp.stack([c_re.reshape(G, C * N), ncim]).reshape(2, SSM_COLSETS, 16, C, N)
    c_exp = jnp.einsum('psgcn,gh->sphngc', cc, eye)
    c_exp = c_exp.reshape(SSM_COLSETS, 2 * COLSET_ST, COLSET_CH).astype(BF16)
    return a_all, b_exp, c_exp


def _cmul(ar, ai, br, bi):
    return ar * br - ai * bi, ar * bi + ai * br


def _ssm_kernel(u_ref, b_ref, c_ref, a_ref, d_ref, y_ref, bu_scr, a_scr, carry_scr):
    ch = pl.program_id(2)
    S = COLSET_ST

    @pl.when(ch == 0)
    def _():
        carry_scr[...] = jnp.zeros_like(carry_scr)

    a_scr[...] = jnp.broadcast_to(a_ref[0], (SUBLANES, 2 * S))

    def abar():
        return a_scr[:, :S], a_scr[:, S:]

    pw = [abar()]
    for _ in range(SC_TAU.bit_length() + 1):
        pw.append(_cmul(*pw[-1], *pw[-1]))
    p1, p2, p4 = pw[-3:]
    sub = lax.broadcasted_iota(jnp.int32, (SUBLANES, S), 0)

    def shifted(v, k):
        return jnp.where(sub >= k, pltpu.roll(v, k, 0), 0.0)

    def tile(base, t):
        return slice(base + t * SUBLANES, base + (t + 1) * SUBLANES)

    def scan(base, hr, hi, store):
        for t in range(1, SC_TAU):
            rows = tile(base, t)
            ar, ai = abar()
            hr, hi = (ar * hr - ai * hi + bu_scr[rows, :S],
                      ar * hi + ai * hr + bu_scr[rows, S:])
            if store:
                bu_scr[rows, :S] = hr
                bu_scr[rows, S:] = hi
        return hr, hi

    u = jnp.concatenate([u_ref[0], u_ref[1]], axis=1)
    u_bf = u.astype(BF16)

    def project_in(k):
        rows = slice(k * SC_LEN, (k + 1) * SC_LEN)
        bu_scr[rows, :] = jnp.dot(u_bf[rows], b_ref[0], preferred_element_type=F32)

    def project_out(k):
        rows = slice(k * SC_LEN, (k + 1) * SC_LEN)
        y = jnp.dot(bu_scr[rows, :].astype(BF16), c_ref[0], preferred_element_type=F32)
        y = jax.nn.gelu(y + d_ref[...] * u[rows])
        y_ref[0, rows, :] = y[:, :LANES]
        y_ref[1, rows, :] = y[:, LANES:]

    cr, ci = carry_scr[:, :S], carry_scr[:, S:]
    project_in(0)
    for k in range(SSM_SC):
        if k + 1 < SSM_SC:
            project_in(k + 1)
        base = k * SC_LEN
        first = tile(base, 0)
        er, ei = scan(base, bu_scr[first, :S], bu_scr[first, S:], False)
        xr = jnp.where(sub == 0, cr, pltpu.roll(er, 1, 0))
        xi = jnp.where(sub == 0, ci, pltpu.roll(ei, 1, 0))
        for (pr, pi), sh in ((p1, 1), (p2, 2), (p4, 4)):
            tr, ti = _cmul(pr, pi, shifted(xr, sh), shifted(xi, sh))
            xr, xi = xr + tr, xi + ti
        tr, ti = _cmul(*p1, xr, xi)
        cr, ci = pltpu.roll(tr + er, 1, 0), pltpu.roll(ti + ei, 1, 0)
        tr, ti = _cmul(*abar(), xr, xi)
        hr, hi = bu_scr[first, :S] + tr, bu_scr[first, S:] + ti
        bu_scr[first, :S] = hr
        bu_scr[first, S:] = hi
        scan(base, hr, hi, True)
        project_out(k)
    carry_scr[:, :S] = cr
    carry_scr[:, S:] = ci


def _ssm(u_il, a_all, b_exp, c_exp, d_flat):
    per_batch = SEQ // SSM_ROWS
    u_spec = pl.BlockSpec((2, SSM_ROWS, LANES), lambda s, b, c: (s, b * per_batch + c, 0))
    state = pltpu.VMEM((SUBLANES, 2 * COLSET_ST), F32)
    return pl.pallas_call(
        _ssm_kernel,
        grid=(SSM_COLSETS, BATCH, per_batch),
        in_specs=[
            u_spec,
            pl.BlockSpec((1, COLSET_CH, 2 * COLSET_ST), lambda s, b, c: (s, 0, 0)),
            pl.BlockSpec((1, 2 * COLSET_ST, COLSET_CH), lambda s, b, c: (s, 0, 0)),
            pl.BlockSpec((1, 1, 2 * COLSET_ST), lambda s, b, c: (s, 0, 0)),
            pl.BlockSpec((1, COLSET_CH), lambda s, b, c: (0, s)),
        ],
        out_specs=u_spec,
        out_shape=jax.ShapeDtypeStruct((SSM_LB, TOKENS, LANES), F32),
        scratch_shapes=[pltpu.VMEM((SSM_ROWS, 2 * COLSET_ST), F32), state, state],
        compiler_params=pltpu.CompilerParams(
            dimension_semantics=("parallel", "parallel", "arbitrary"),
            vmem_limit_bytes=VMEM_LIMIT),
        name="ssm",
    )(u_il, b_exp, c_exp, a_all, d_flat)


def _branch_distances():
    out = []
    for dil in BRANCH_DILATIONS:
        per = 32 if dil == 1 else BRANCH_BLOCK
        idx = np.arange(BRANCH_BLOCK)
        pos = (BRANCH_BLOCK // per) * (idx % per) + idx // per
        dist = np.concatenate([pos[:, None] - pos[None, :] + BRANCH_BLOCK,
                               pos[:, None] - pos[None, :]], axis=1)
        ok = (dist >= 0) & (dist <= BRANCH_BLOCK)
        out.append(np.where(ok, dil * dist, -1).astype(np.float32))
    return np.stack(out)


def _attn_tile(qt, kt, vt, bm):
    s = lax.dot_general(qt, kt, (((1,), (1,)), ((), ())), preferred_element_type=F32)
    s = jnp.where(bm > 0.5 * MASK_VALUE, s + bm, MASK_VALUE)
    m = jnp.max(s, axis=-1, keepdims=True)
    p = jnp.exp(s - m)
    l = jnp.sum(p, axis=-1, keepdims=True)
    o = jnp.dot(p.astype(BF16), vt, preferred_element_type=F32)
    return m, l, o


def _attn_kernel(q_ref, k_ref, v_ref, dd_ref, o_ref,
                 nat_scr, q4, k4, v4, m4, l4, o4, bias_scr):
    h = pl.program_id(1)
    i = pl.program_id(2)
    Q = ATT_Q
    E = HEAD_DIM
    BLK = BRANCH_BLOCK
    cur = pl.multiple_of((i % 2) * Q, Q)
    prv = pl.multiple_of(Q - (i % 2) * Q, Q)

    def to_slabs(src_ref, dst, base):
        nat_scr[...] = src_ref[0, 0].astype(F32)
        for r in range(4):
            dst[r, pl.ds(base, Q), :] = nat_scr[pl.ds(r, Q, stride=4), :]

    to_slabs(q_ref, q4, 0)
    to_slabs(k_ref, k4, cur)
    to_slabs(v_ref, v4, cur)

    @pl.when(i == 0)
    def _():
        for r in range(4):
            k4[r, pl.ds(prv, Q), :] = jnp.zeros((Q, E), F32)
            v4[r, pl.ds(prv, Q), :] = jnp.zeros((Q, E), F32)

    hv = jnp.full((BLK, 2 * BLK), h, jnp.int32).astype(F32)
    slope = jnp.exp2(-8.0 * (hv + 1.0) / N_HEADS)
    is_prev = lax.broadcasted_iota(jnp.int32, (BLK, 2 * BLK), 1) < BLK
    for br in range(3):
        dd = dd_ref[br]
        bm = jnp.where(dd >= 0.0, (-slope) * dd, MASK_VALUE)
        bias_scr[2 * br] = bm
        bias_scr[2 * br + 1] = jnp.where(is_prev, MASK_VALUE, bm)

    def load(scr, chunks):
        return jnp.concatenate([scr[c] for c in chunks], axis=0)

    def merge(chunks, nrows, m_t, l_t, o_t, first, last):
        for n, c in enumerate(chunks):
            sl = slice(n * nrows, (n + 1) * nrows)
            m_c = jnp.broadcast_to(m_t[sl], (nrows, E))
            l_c = jnp.broadcast_to(l_t[sl], (nrows, E))
            o_c = o_t[sl]
            if first:
                m4[c] = m_c
                l4[c] = l_c
                o4[c] = o_c
                continue
            m_old = m4[c]
            m_new = jnp.maximum(m_old, m_c)
            alpha = jnp.exp(m_old - m_new)
            beta = jnp.exp(m_c - m_new)
            l_new = alpha * l4[c] + beta * l_c
            o_new = alpha * o4[c] + beta * o_c
            if last:
                o4[c] = o_new / l_new
            else:
                m4[c] = m_new
                l4[c] = l_new
                o4[c] = o_new

    def run_tile(br, q_chunks, kp_chunks, kc_chunks, nrows, seq_start):
        qt = load(q4, q_chunks).astype(BF16)
        kt = jnp.concatenate([load(k4, kp_chunks), load(k4, kc_chunks)], axis=0).astype(BF16)
        vt = jnp.concatenate([load(v4, kp_chunks), load(v4, kc_chunks)], axis=0).astype(BF16)
        if seq_start:
            bm = bias_scr[2 * br + (i == 0).astype(jnp.int32)]
        else:
            bm = bias_scr[2 * br]
        m_t, l_t, o_t = _attn_tile(qt, kt, vt, bm)
        merge(q_chunks, nrows, m_t, l_t, o_t, br == 0, br == 2)

    for r in range(4):
        for r2 in range(4):
            run_tile(0,
                     [(r, pl.ds(r2, BLK, stride=4), slice(None))],
                     [(r, pl.ds(prv + r2, BLK, stride=4), slice(None))],
                     [(r, pl.ds(cur + r2, BLK, stride=4), slice(None))],
                     BLK, True)

    for r in range(4):
        for b in range(4):
            kp = prv + (Q - BLK) if b == 0 else cur + (b - 1) * BLK
            run_tile(1,
                     [(r, pl.ds(b * BLK, BLK), slice(None))],
                     [(r, pl.ds(pl.multiple_of(kp, BLK), BLK), slice(None))],
                     [(r, pl.ds(pl.multiple_of(cur + b * BLK, BLK), BLK), slice(None))],
                     BLK, b == 0)

    for b in range(SEG_LEN // BLK):
        kp = prv + (Q - 32) if b == 0 else cur + (b - 1) * 32
        run_tile(2,
                 [(r, pl.ds(b * 32, 32), slice(None)) for r in range(4)],
                 [(r, pl.ds(pl.multiple_of(kp, 32), 32), slice(None)) for r in range(4)],
                 [(r, pl.ds(pl.multiple_of(cur + b * 32, 32), 32), slice(None))
                  for r in range(4)],
                 32, b == 0)

    for r in range(4):
        nat_scr[pl.ds(r, Q, stride=4), :] = o4[r]
    o_ref[0, 0] = nat_scr[...].astype(BF16)


def _attention(q, k, v):
    blk = pl.BlockSpec((1, 1, SEG_LEN, HEAD_DIM),
                       lambda b, h, i: (b * SEG_PER_BATCH + i, h, 0, 0))
    dd = jnp.asarray(_branch_distances())
    slab = pltpu.VMEM((4, ATT_Q, HEAD_DIM), F32)
    slab2 = pltpu.VMEM((4, 2 * ATT_Q, HEAD_DIM), F32)
    return pl.pallas_call(
        _attn_kernel,
        grid=(BATCH, N_HEADS, SEG_PER_BATCH),
        in_specs=[blk, blk, blk,
                  pl.BlockSpec((3, BRANCH_BLOCK, 2 * BRANCH_BLOCK), lambda b, h, i: (0, 0, 0))],
        out_specs=blk,
        out_shape=jax.ShapeDtypeStruct((N_SEG, N_HEADS, SEG_LEN, HEAD_DIM), BF16),
        scratch_shapes=[
            pltpu.VMEM((SEG_LEN, HEAD_DIM), F32),
            slab, slab2, slab2,
            slab, slab, slab,
            pltpu.VMEM((6, BRANCH_BLOCK, 2 * BRANCH_BLOCK), F32),
        ],
        compiler_params=pltpu.CompilerParams(
            dimension_semantics=("parallel", "parallel", "arbitrary"),
            vmem_limit_bytes=VMEM_LIMIT),
        name="attn",
    )(q, k, v, dd)


def _mix_kernel(x_ref, y_ref, att_ref, g_ref, wglu_ref, wout_ref, o_ref, cat_scr):
    y = jnp.concatenate(
        [jnp.concatenate([y_ref[lb, pl.ds(start, SC_TAU, stride=SUBLANES), :]
                          for _, start in _interleaved_pieces()], axis=0)
         for lb in range(SSM_LB)], axis=1)
    gl = jnp.dot(y.astype(BF16), wglu_ref[...], preferred_element_type=F32)
    cat_scr[:, :SSM_WIDTH] = (y * _sigmoid(gl)).astype(BF16)
    for h in range(N_HEADS):
        c0 = SSM_WIDTH + h * HEAD_DIM
        cat_scr[:, c0:c0 + HEAD_DIM] = att_ref[0, h]
    mixed = jnp.dot(cat_scr[...], wout_ref[...], preferred_element_type=F32)
    o_ref[...] = x_ref[...] + g_ref[0] * mixed


def _mix(x2, y_il, att, gate1, w_glu_bf, w_out_bf):
    x_spec = pl.BlockSpec((ROWS, D_MODEL), lambda i: (i, 0))
    return pl.pallas_call(
        _mix_kernel,
        grid=(TOKENS // ROWS,),
        in_specs=[
            x_spec,
            pl.BlockSpec((SSM_LB, ROWS, LANES), lambda i: (0, i, 0)),
            pl.BlockSpec((1, N_HEADS, ROWS, HEAD_DIM),
                         lambda i: (i // BLK_PER_SEG, 0, i % BLK_PER_SEG, 0)),
            pl.BlockSpec((1, 1, D_MODEL), lambda i: (i // BLK_PER_BATCH, 0, 0)),
            pl.BlockSpec((SSM_WIDTH, SSM_WIDTH), lambda i: (0, 0)),
            pl.BlockSpec((D_MODEL, D_MODEL), lambda i: (0, 0)),
        ],
        out_specs=x_spec,
        out_shape=jax.ShapeDtypeStruct((TOKENS, D_MODEL), F32),
        scratch_shapes=[pltpu.VMEM((ROWS, D_MODEL), BF16)],
        compiler_params=pltpu.CompilerParams(
            dimension_semantics=("parallel",), vmem_limit_bytes=VMEM_LIMIT),
        name="mix",
    )(x2, y_il, att, gate1, w_glu_bf, w_out_bf)


def _ffn_kernel(x_ref, sh_ref, sc_ref, g_ref, nw_ref, wg_ref, wu_ref, wd_ref, o_ref, h_scr):
    j = pl.program_id(1)

    def hidden_tile(rows):
        h = h_scr[rows, :]
        g = jnp.dot(h, wg_ref[...], preferred_element_type=F32)
        u = jnp.dot(h, wu_ref[...], preferred_element_type=F32)
        a = (g * _sigmoid(g) * u).astype(BF16)
        return jnp.dot(a, wd_ref[...], preferred_element_type=F32)

    @pl.when(j == 0)
    def _():
        sub = FFN_TM // FFN_PROLOGUE_SPLIT
        for r in range(FFN_PROLOGUE_SPLIT):
            rows = slice(r * sub, (r + 1) * sub)
            h_scr[rows, :] = _modulated_norm(x_ref[rows, :], nw_ref[...], sh_ref[0], sc_ref[0])
        for r in range(FFN_PROLOGUE_SPLIT):
            rows = slice(r * sub, (r + 1) * sub)
            o_ref[rows, :] = hidden_tile(rows)

    @pl.when(j > 0)
    def _():
        o_ref[...] += hidden_tile(slice(None))

    @pl.when(j == FFN_HIDDEN // FFN_TH - 1)
    def _():
        o_ref[...] = x_ref[...] + g_ref[0] * o_ref[...]


def _ffn(x1, shift2, scale2, gate2, norm2_w, wg_bf, wu_bf, wd_bf):
    tm, th = FFN_TM, FFN_TH
    per_batch = SEQ // tm
    mod_spec = pl.BlockSpec((1, 1, D_MODEL), lambda i, j: (i // per_batch, 0, 0))
    return pl.pallas_call(
        _ffn_kernel,
        grid=(TOKENS // tm, FFN_HIDDEN // th),
        in_specs=[
            pl.BlockSpec((tm, D_MODEL), lambda i, j: (i, 0), pipeline_mode=pl.Buffered(1)),
            mod_spec, mod_spec, mod_spec,
            pl.BlockSpec((1, D_MODEL), lambda i, j: (0, 0)),
            pl.BlockSpec((D_MODEL, th), lambda i, j: (0, j)),
            pl.BlockSpec((D_MODEL, th), lambda i, j: (0, j)),
            pl.BlockSpec((th, D_MODEL), lambda i, j: (j, 0)),
        ],
        out_specs=pl.BlockSpec((tm, D_MODEL), lambda i, j: (i, 0)),
        out_shape=jax.ShapeDtypeStruct((TOKENS, D_MODEL), F32),
        scratch_shapes=[pltpu.VMEM((tm, D_MODEL), BF16)],
        compiler_params=pltpu.CompilerParams(
            dimension_semantics=("parallel", "arbitrary"), vmem_limit_bytes=VMEM_LIMIT),
        name="ffn",
    )(x1, shift2, scale2, gate2, norm2_w, wg_bf, wu_bf, wd_bf)


def kernel(x, c, w_ada, b_ada, norm1_w, w_in, ssm_a_re, ssm_a_im, ssm_log_dt, ssm_b_re, ssm_b_im,
           ssm_c_re, ssm_c_im, ssm_d, ssm_w_glu, q_norm_w, k_norm_w, w_out, norm2_w,
           w_ffn_gate, w_ffn_up, w_ffn_down):
    x2 = x.reshape(TOKENS, D_MODEL)
    for i in range(w_ada.shape[0]):
        mod = _ada(c, w_ada[i], b_ada[i])
        shift1, scale1, gate1, shift2, scale2, gate2 = [
            m.reshape(BATCH, 1, D_MODEL) for m in jnp.split(mod, N_MOD, axis=-1)]

        u_il, q, k, v = _inproj(x2, shift1, scale1, norm1_w[i].reshape(1, D_MODEL),
                                w_in[i].astype(BF16),
                                q_norm_w[i].reshape(1, HEAD_DIM), k_norm_w[i].reshape(1, HEAD_DIM))

        a_all, b_exp, c_exp = _ssm_prep(ssm_a_re[i], ssm_a_im[i], ssm_log_dt[i], ssm_b_re[i],
                                        ssm_b_im[i], ssm_c_re[i], ssm_c_im[i])
        y_il = _ssm(u_il, a_all, b_exp, c_exp, ssm_d[i].reshape(1, SSM_WIDTH))
        att = _attention(q, k, v)

        x2 = _mix(x2, y_il, att, gate1, ssm_w_glu[i].astype(BF16), w_out[i].astype(BF16))
        x2 = _ffn(x2, shift2, scale2, gate2,
                  norm2_w[i].reshape(1, D_MODEL), w_ffn_gate[i].astype(BF16),
                  w_ffn_up[i].astype(BF16), w_ffn_down[i].astype(BF16))
    return x2.reshape(BATCH, SEQ, D_MODEL)
```

```python
import numpy as np
import jax
import jax.numpy as jnp
from jax import lax
from jax.experimental import pallas as pl
from jax.experimental.pallas import tpu as pltpu

F32 = jnp.float32
BF16 = jnp.bfloat16

D_MODEL = 2048
BATCH = 2
SEQ = 8192
TOKENS = BATCH * SEQ
SSM_WIDTH = 1024
SSM_GROUP = 16
SSM_GROUPS = 64
SSM_STATE = 64
ATTN_WIDTH = 1024
HEAD_DIM = 128
N_HEADS = 8
BRANCH_DILATIONS = (16, 4, 1)
BRANCH_BLOCK = 128
FFN_HIDDEN = 5632
N_MOD = 6
NORM_EPS = 1e-6
MASK_VALUE = -1e30
LANES = 128

N_SEG = 8
SEG_LEN = TOKENS // N_SEG
SEG_PER_BATCH = SEQ // SEG_LEN
ROWS = 512
BLK_PER_SEG = SEG_LEN // ROWS
BLK_PER_BATCH = SEQ // ROWS

SUBLANES = 8
SC_TAU = 32
SC_LEN = SUBLANES * SC_TAU
SC_PER_ROWBLK = ROWS // SC_LEN
SSM_LB = SSM_WIDTH // LANES
SSM_COLSETS = 4
COLSET_CH = SSM_WIDTH // SSM_COLSETS
COLSET_ST = 16 * SSM_STATE

ATT_Q = SEG_LEN // 4

ADA_TN = 1024
FFN_TM = 1024
FFN_TH = 512
FFN_PROLOGUE_SPLIT = 4

VMEM_LIMIT = 56 * 1024 * 1024
FRONT_VMEM_LIMIT = 60 * 1024 * 1024


def _sigmoid(x):
    return 1.0 / (1.0 + jnp.exp(-x))


def _modulated_norm(x, nw, shift, scale):
    ms = jnp.mean(x * x, axis=-1, keepdims=True)
    y = x * lax.rsqrt(ms + NORM_EPS) * nw
    return (y * (1.0 + scale) + shift).astype(BF16)


def _interleaved_pieces():
    return [(slice(sc * SC_LEN + s * SC_TAU, sc * SC_LEN + (s + 1) * SC_TAU), sc * SC_LEN + s)
            for sc in range(SC_PER_ROWBLK) for s in range(SUBLANES)]


def _ada_kernel(ct_ref, w_ref, b_ref, o_ref, cond_scr):
    @pl.when(pl.program_id(0) == 0)
    def _():
        c = ct_ref[...]
        cond = c * _sigmoid(c)
        for b in range(BATCH):
            cond_scr[b] = jnp.broadcast_to(cond[:, b:b + 1], (D_MODEL, LANES))

    nlb = ADA_TN // LANES

    def body(kc, acc):
        row = pl.multiple_of(kc * 8, 8)
        out = []
        for b in range(BATCH):
            ck = cond_scr[b, pl.ds(row, 8), :]
            for lb in range(nlb):
                wk = w_ref[pl.ds(row, 8), lb * LANES:(lb + 1) * LANES]
                out.append(acc[b * nlb + lb] + ck * wk)
        return tuple(out)

    zero = jnp.zeros((8, LANES), F32)
    acc = lax.fori_loop(0, D_MODEL // 8, body, (zero,) * (BATCH * nlb), unroll=8)
    rows = [jnp.concatenate([jnp.sum(acc[b * nlb + lb], axis=0, keepdims=True)
                             for lb in range(nlb)], axis=1) for b in range(BATCH)]
    o_ref[...] = jnp.concatenate(rows, axis=0) + b_ref[...]


def _ada(c, w_ada, b_ada):
    n = N_MOD * D_MODEL
    return pl.pallas_call(
        _ada_kernel,
        grid=(n // ADA_TN,),
        in_specs=[
            pl.BlockSpec((D_MODEL, BATCH), lambda j: (0, 0)),
            pl.BlockSpec((D_MODEL, ADA_TN), lambda j: (0, j)),
            pl.BlockSpec((1, ADA_TN), lambda j: (0, j)),
        ],
        out_specs=pl.BlockSpec((BATCH, ADA_TN), lambda j: (0, j)),
        out_shape=jax.ShapeDtypeStruct((BATCH, n), F32),
        scratch_shapes=[pltpu.VMEM((BATCH, D_MODEL, LANES), F32)],
        compiler_params=pltpu.CompilerParams(
            dimension_semantics=("arbitrary",), vmem_limit_bytes=VMEM_LIMIT),
        name="ada",
    )(c.T, w_ada, b_ada.reshape(1, n))


def _head_rms(blk, w):
    ms = jnp.mean(blk * blk, axis=-1, keepdims=True)
    return blk * lax.rsqrt(ms + NORM_EPS) * w


def _front_kernel(x_ref, sh_ref, sc_ref, nw_ref, w_ref, qw_ref, kw_ref, b_ref, c_ref, a_ref, d_ref,
                  y_ref, q_ref, k_ref, v_ref, h_scr, u_scr, bu_scr, a_scr, carry_scr):
    @pl.when(pl.program_id(0) % BLK_PER_BATCH == 0)
    def _():
        carry_scr[...] = jnp.zeros_like(carry_scr)

    h_scr[...] = _modulated_norm(x_ref[...], nw_ref[...], sh_ref[0], sc_ref[0])
    h = h_scr[...]

    def proj(n):
        return jnp.dot(h, w_ref[:, n * 1024:(n + 1) * 1024], preferred_element_type=F32)

    def per_head(res, out_ref, fn):
        for hd in range(N_HEADS):
            out_ref[0, hd] = fn(res[:, hd * HEAD_DIM:(hd + 1) * HEAD_DIM]).astype(BF16)

    res = proj(0)
    for lb in range(SSM_LB):
        for rows, start in _interleaved_pieces():
            u_scr[lb, pl.ds(start, SC_TAU, stride=SUBLANES), :] = (
                res[rows, lb * LANES:(lb + 1) * LANES])

    heads = (
        lambda: per_head(proj(1), q_ref,
                         lambda blk: _head_rms(blk, qw_ref[...]) * (HEAD_DIM ** -0.5)),
        lambda: per_head(proj(2), k_ref, lambda blk: _head_rms(blk, kw_ref[...])),
        lambda: per_head(proj(3), v_ref, lambda blk: blk),
    )
    for s in range(SSM_COLSETS):
        def store_y(rows, y, s=s):
            y_ref[2 * s, rows, :] = y[:, :LANES]
            y_ref[2 * s + 1, rows, :] = y[:, LANES:]

        u = jnp.concatenate([u_scr[2 * s], u_scr[2 * s + 1]], axis=1)
        _ssm_block(u, b_ref.at[s], c_ref.at[s], a_ref[s],
                   d_ref[:, s * COLSET_CH:(s + 1) * COLSET_CH], store_y,
                   bu_scr.at[s % 2], a_scr.at[s], carry_scr.at[s])
        if s < len(heads):
            heads[s]()


def _front(x2, shift1, scale1, norm1_w, w_in_bf, q_norm_w, k_norm_w, a_all, b_exp, c_exp, d_flat):
    qkv_shape = jax.ShapeDtypeStruct((N_SEG, N_HEADS, SEG_LEN, HEAD_DIM), BF16)
    qkv_spec = pl.BlockSpec((1, N_HEADS, ROWS, HEAD_DIM),
                            lambda i: (i // BLK_PER_SEG, 0, i % BLK_PER_SEG, 0))
    mod_spec = pl.BlockSpec((1, 1, D_MODEL), lambda i: (i // BLK_PER_BATCH, 0, 0))

    def resident(shape):
        return pl.BlockSpec(shape, lambda i: (0,) * len(shape), pipeline_mode=pl.Buffered(1))

    states = pltpu.VMEM((SSM_COLSETS, SUBLANES, 2 * COLSET_ST), F32)
    return pl.pallas_call(
        _front_kernel,
        grid=(TOKENS // ROWS,),
        in_specs=[
            pl.BlockSpec((ROWS, D_MODEL), lambda i: (i, 0)),
            mod_spec, mod_spec,
            pl.BlockSpec((1, D_MODEL), lambda i: (0, 0)),
            resident((D_MODEL, 4 * 1024)),
            pl.BlockSpec((1, HEAD_DIM), lambda i: (0, 0)),
            pl.BlockSpec((1, HEAD_DIM), lambda i: (0, 0)),
            resident((SSM_COLSETS, COLSET_CH, 2 * COLSET_ST)),
            resident((SSM_COLSETS, 2 * COLSET_ST, COLSET_CH)),
            pl.BlockSpec((SSM_COLSETS, 1, 2 * COLSET_ST), lambda i: (0, 0, 0)),
            pl.BlockSpec((1, SSM_WIDTH), lambda i: (0, 0)),
        ],
        out_specs=[
            pl.BlockSpec((SSM_LB, ROWS, LANES), lambda i: (0, i, 0)),
            qkv_spec, qkv_spec, qkv_spec,
        ],
        out_shape=[
            jax.ShapeDtypeStruct((SSM_LB, TOKENS, LANES), F32),
            qkv_shape, qkv_shape, qkv_shape,
        ],
        scratch_shapes=[
            pltpu.VMEM((ROWS, D_MODEL), BF16),
            pltpu.VMEM((SSM_LB, ROWS, LANES), F32),
            pltpu.VMEM((2, ROWS, 2 * COLSET_ST), F32),
            states, states,
        ],
        compiler_params=pltpu.CompilerParams(
            dimension_semantics=("arbitrary",), vmem_limit_bytes=FRONT_VMEM_LIMIT),
        name="front",
    )(x2, shift1, scale1, norm1_w, w_in_bf, q_norm_w, k_norm_w, b_exp, c_exp, a_all, d_flat)


def _ssm_prep_kernel(are_ref, aim_ref, ldt_ref, bre_ref, bim_ref, cim_ref,
                     abr_ref, abi_ref, bbr_ref, bbi_ref, ncim_ref):
    a_re = are_ref[...]
    a_im = aim_ref[...]
    dt = jnp.exp(ldt_ref[...])
    mag = jnp.exp(a_re * dt)
    abar_re = mag * jnp.cos(a_im * dt)
    abar_im = mag * jnp.sin(a_im * dt)
    num_re = abar_re - 1.0
    num_im = abar_im
    den = a_re * a_re + a_im * a_im
    z_re = (num_re * a_re + num_im * a_im) / den
    z_im = (num_im * a_re - num_re * a_im) / den
    b_re = bre_ref[...]
    b_im = bim_ref[...]
    abr_ref[...] = abar_re
    abi_ref[...] = abar_im
    bbr_ref[...] = z_re * b_re - z_im * b_im
    bbi_ref[...] = z_re * b_im + z_im * b_re
    ncim_ref[...] = -cim_ref[...]


def _ssm_prep(a_re, a_im, log_dt, b_re, b_im, c_re, c_im):
    G, N, C = SSM_GROUPS, SSM_STATE, SSM_GROUP
    tile_n = lambda a: jnp.tile(a, (1, C))
    cn = lambda b: b.transpose(0, 2, 1).reshape(G, C * N)
    shp = jax.ShapeDtypeStruct((G, C * N), F32)
    abr, abi, bbr, bbi, ncim = pl.pallas_call(
        _ssm_prep_kernel,
        out_shape=[shp] * 5,
        name="ssm_prep",
    )(tile_n(a_re), tile_n(a_im), log_dt.reshape(G, 1), cn(b_re), cn(b_im),
      c_im.reshape(G, C * N))

    eye = jnp.eye(16, dtype=F32)
    a_all = jnp.stack([abr[:, :N], abi[:, :N]])
    a_all = a_all.reshape(2, SSM_COLSETS, 16 * N).transpose(1, 0, 2)
    a_all = a_all.reshape(SSM_COLSETS, 1, 2 * COLSET_ST)
    bb = jnp.stack([bbr, bbi]).reshape(2, SSM_COLSETS, 16, C, N)
    b_exp = jnp.einsum('psgcn,gh->sgcphn', bb, eye)
    b_exp = b_exp.reshape(SSM_COLSETS, COLSET_CH, 2 * COLSET_ST).astype(BF16)
    cc = jnp.stack([c_re.reshape(G, C * N), ncim]).reshape(2, SSM_COLSETS, 16, C, N)
    c_exp = jnp.einsum('psgcn,gh->sphngc', cc, eye)
    c_exp = c_exp.reshape(SSM_COLSETS, 2 * COLSET_ST, COLSET_CH).astype(BF16)
    return a_all, b_exp, c_exp


def _cmul(ar, ai, br, bi):
    return ar * br - ai * bi, ar * bi + ai * br


def _ssm_block(u, b_ref, c_ref, a_row, d_row, store_y, bu_scr, a_scr, carry_scr):
    S = COLSET_ST
    a_scr[...] = jnp.broadcast_to(a_row, (SUBLANES, 2 * S))

    def abar():
        return a_scr[:, :S], a_scr[:, S:]

    pw = [abar()]
    for _ in range(SC_TAU.bit_length() + 1):
        pw.append(_cmul(*pw[-1], *pw[-1]))
    p1, p2, p4 = pw[-3:]
    sub = lax.broadcasted_iota(jnp.int32, (SUBLANES, S), 0)

    def shifted(v, k):
        return jnp.where(sub >= k, pltpu.roll(v, k, 0), 0.0)

    def tile(base, t):
        return slice(base + t * SUBLANES, base + (t + 1) * SUBLANES)

    def scan(base, hr, hi, store):
        for t in range(1, SC_TAU):
            rows = tile(base, t)
            ar, ai = abar()
            hr, hi = (ar * hr - ai * hi + bu_scr[rows, :S],
                      ar * hi + ai * hr + bu_scr[rows, S:])
            if store:
                bu_scr[rows, :S] = hr
                bu_scr[rows, S:] = hi
        return hr, hi

    u_bf = u.astype(BF16)

    def project_in(k):
        rows = slice(k * SC_LEN, (k + 1) * SC_LEN)
        bu_scr[rows, :] = jnp.dot(u_bf[rows], b_ref[...], preferred_element_type=F32)

    def project_out(k):
        rows = slice(k * SC_LEN, (k + 1) * SC_LEN)
        y = jnp.dot(bu_scr[rows, :].astype(BF16), c_ref[...], preferred_element_type=F32)
        store_y(rows, jax.nn.gelu(y + d_row * u[rows]))

    cr, ci = carry_scr[:, :S], carry_scr[:, S:]
    project_in(0)
    for k in range(SC_PER_ROWBLK):
        if k + 1 < SC_PER_ROWBLK:
            project_in(k + 1)
        base = k * SC_LEN
        first = tile(base, 0)
        er, ei = scan(base, bu_scr[first, :S], bu_scr[first, S:], False)
        xr = jnp.where(sub == 0, cr, pltpu.roll(er, 1, 0))
        xi = jnp.where(sub == 0, ci, pltpu.roll(ei, 1, 0))
        for (pr, pi), sh in ((p1, 1), (p2, 2), (p4, 4)):
            tr, ti = _cmul(pr, pi, shifted(xr, sh), shifted(xi, sh))
            xr, xi = xr + tr, xi + ti
        tr, ti = _cmul(*p1, xr, xi)
        cr, ci = pltpu.roll(tr + er, 1, 0), pltpu.roll(ti + ei, 1, 0)
        tr, ti = _cmul(*abar(), xr, xi)
        hr, hi = bu_scr[first, :S] + tr, bu_scr[first, S:] + ti
        bu_scr[first, :S] = hr
        bu_scr[first, S:] = hi
        scan(base, hr, hi, True)
        project_out(k)
    carry_scr[:, :S] = cr
    carry_scr[:, S:] = ci


def _branch_distances():
    out = []
    for dil in BRANCH_DILATIONS:
        per = 32 if dil == 1 else BRANCH_BLOCK
        idx = np.arange(BRANCH_BLOCK)
        pos = (BRANCH_BLOCK // per) * (idx % per) + idx // per
        dist = np.concatenate([pos[:, None] - pos[None, :] + BRANCH_BLOCK,
                               pos[:, None] - pos[None, :]], axis=1)
        ok = (dist >= 0) & (dist <= BRANCH_BLOCK)
        out.append(np.where(ok, dil * dist, -1).astype(np.float32))
    return np.stack(out)


def _attn_tile(qt, kt, vt, bm):
    s = lax.dot_general(qt, kt, (((1,), (1,)), ((), ())), preferred_element_type=F32)
    s = jnp.where(bm > 0.5 * MASK_VALUE, s + bm, MASK_VALUE)
    m = jnp.max(s, axis=-1, keepdims=True)
    p = jnp.exp(s - m)
    l = jnp.sum(p, axis=-1, keepdims=True)
    o = jnp.dot(p.astype(BF16), vt, preferred_element_type=F32)
    return m, l, o


def _attn_kernel(q_ref, k_ref, v_ref, dd_ref, o_ref,
                 nat_scr, q4, k4, v4, m4, l4, o4, bias_scr):
    h = pl.program_id(1)
    i = pl.program_id(2)
    Q = ATT_Q
    E = HEAD_DIM
    BLK = BRANCH_BLOCK
    cur = pl.multiple_of((i % 2) * Q, Q)
    prv = pl.multiple_of(Q - (i % 2) * Q, Q)

    def to_slabs(src_ref, dst, base):
        nat_scr[...] = src_ref[0, 0].astype(F32)
        for r in range(4):
            dst[r, pl.ds(base, Q), :] = nat_scr[pl.ds(r, Q, stride=4), :]

    to_slabs(q_ref, q4, 0)
    to_slabs(k_ref, k4, cur)
    to_slabs(v_ref, v4, cur)

    @pl.when(i == 0)
    def _():
        for r in range(4):
            k4[r, pl.ds(prv, Q), :] = jnp.zeros((Q, E), F32)
            v4[r, pl.ds(prv, Q), :] = jnp.zeros((Q, E), F32)

    hv = jnp.full((BLK, 2 * BLK), h, jnp.int32).astype(F32)
    slope = jnp.exp2(-8.0 * (hv + 1.0) / N_HEADS)
    is_prev = lax.broadcasted_iota(jnp.int32, (BLK, 2 * BLK), 1) < BLK
    for br in range(3):
        dd = dd_ref[br]
        bm = jnp.where(dd >= 0.0, (-slope) * dd, MASK_VALUE)
        bias_scr[2 * br] = bm
        bias_scr[2 * br + 1] = jnp.where(is_prev, MASK_VALUE, bm)

    def load(scr, chunks):
        return jnp.concatenate([scr[c] for c in chunks], axis=0)

    def merge(chunks, nrows, m_t, l_t, o_t, first, last):
        for n, c in enumerate(chunks):
            sl = slice(n * nrows, (n + 1) * nrows)
            m_c = jnp.broadcast_to(m_t[sl], (nrows, E))
            l_c = jnp.broadcast_to(l_t[sl], (nrows, E))
            o_c = o_t[sl]
            if first:
                m4[c] = m_c
                l4[c] = l_c
                o4[c] = o_c
                continue
            m_old = m4[c]
            m_new = jnp.maximum(m_old, m_c)
            alpha = jnp.exp(m_old - m_new)
            beta = jnp.exp(m_c - m_new)
            l_new = alpha * l4[c] + beta * l_c
            o_new = alpha * o4[c] + beta * o_c
            if last:
                o4[c] = o_new / l_new
            else:
                m4[c] = m_new
                l4[c] = l_new
                o4[c] = o_new

    def run_tile(br, q_chunks, kp_chunks, kc_chunks, nrows, seq_start):
        qt = load(q4, q_chunks).astype(BF16)
        kt = jnp.concatenate([load(k4, kp_chunks), load(k4, kc_chunks)], axis=0).astype(BF16)
        vt = jnp.concatenate([load(v4, kp_chunks), load(v4, kc_chunks)], axis=0).astype(BF16)
        if seq_start:
            bm = bias_scr[2 * br + (i == 0).astype(jnp.int32)]
        else:
            bm = bias_scr[2 * br]
        m_t, l_t, o_t = _attn_tile(qt, kt, vt, bm)
        merge(q_chunks, nrows, m_t, l_t, o_t, br == 0, br == 2)

    for r in range(4):
        for r2 in range(4):
            run_tile(0,
                     [(r, pl.ds(r2, BLK, stride=4), slice(None))],
                     [(r, pl.ds(prv + r2, BLK, stride=4), slice(None))],
                     [(r, pl.ds(cur + r2, BLK, stride=4), slice(None))],
                     BLK, True)

    for r in range(4):
        for b in range(4):
            kp = prv + (Q - BLK) if b == 0 else cur + (b - 1) * BLK
            run_tile(1,
                     [(r, pl.ds(b * BLK, BLK), slice(None))],
                     [(r, pl.ds(pl.multiple_of(kp, BLK), BLK), slice(None))],
                     [(r, pl.ds(pl.multiple_of(cur + b * BLK, BLK), BLK), slice(None))],
                     BLK, b == 0)

    for b in range(SEG_LEN // BLK):
        kp = prv + (Q - 32) if b == 0 else cur + (b - 1) * 32
        run_tile(2,
                 [(r, pl.ds(b * 32, 32), slice(None)) for r in range(4)],
                 [(r, pl.ds(pl.multiple_of(kp, 32), 32), slice(None)) for r in range(4)],
                 [(r, pl.ds(pl.multiple_of(cur + b * 32, 32), 32), slice(None))
                  for r in range(4)],
                 32, b == 0)

    for r in range(4):
        nat_scr[pl.ds(r, Q, stride=4), :] = o4[r]
    o_ref[0, 0] = nat_scr[...].astype(BF16)


def _attention(q, k, v):
    blk = pl.BlockSpec((1, 1, SEG_LEN, HEAD_DIM),
                       lambda b, h, i: (b * SEG_PER_BATCH + i, h, 0, 0))
    dd = jnp.asarray(_branch_distances())
    slab = pltpu.VMEM((4, ATT_Q, HEAD_DIM), F32)
    slab2 = pltpu.VMEM((4, 2 * ATT_Q, HEAD_DIM), F32)
    return pl.pallas_call(
        _attn_kernel,
        grid=(BATCH, N_HEADS, SEG_PER_BATCH),
        in_specs=[blk, blk, blk,
                  pl.BlockSpec((3, BRANCH_BLOCK, 2 * BRANCH_BLOCK), lambda b, h, i: (0, 0, 0))],
        out_specs=blk,
        out_shape=jax.ShapeDtypeStruct((N_SEG, N_HEADS, SEG_LEN, HEAD_DIM), BF16),
        scratch_shapes=[
            pltpu.VMEM((SEG_LEN, HEAD_DIM), F32),
            slab, slab2, slab2,
            slab, slab, slab,
            pltpu.VMEM((6, BRANCH_BLOCK, 2 * BRANCH_BLOCK), F32),
        ],
        compiler_params=pltpu.CompilerParams(
            dimension_semantics=("parallel", "parallel", "arbitrary"),
            vmem_limit_bytes=VMEM_LIMIT),
        name="attn",
    )(q, k, v, dd)


def _mix_kernel(x_ref, y_ref, att_ref, g_ref, wglu_ref, wout_ref, o_ref, cat_scr):
    y = jnp.concatenate(
        [jnp.concatenate([y_ref[lb, pl.ds(start, SC_TAU, stride=SUBLANES), :]
                          for _, start in _interleaved_pieces()], axis=0)
         for lb in range(SSM_LB)], axis=1)
    gl = jnp.dot(y.astype(BF16), wglu_ref[...], preferred_element_type=F32)
    cat_scr[:, :SSM_WIDTH] = (y * _sigmoid(gl)).astype(BF16)
    for h in range(N_HEADS):
        c0 = SSM_WIDTH + h * HEAD_DIM
        cat_scr[:, c0:c0 + HEAD_DIM] = att_ref[0, h]
    mixed = jnp.dot(cat_scr[...], wout_ref[...], preferred_element_type=F32)
    o_ref[...] = x_ref[...] + g_ref[0] * mixed


def _mix(x2, y_il, att, gate1, w_glu_bf, w_out_bf):
    x_spec = pl.BlockSpec((ROWS, D_MODEL), lambda i: (i, 0))
    return pl.pallas_call(
        _mix_kernel,
        grid=(TOKENS // ROWS,),
        in_specs=[
            x_spec,
            pl.BlockSpec((SSM_LB, ROWS, LANES), lambda i: (0, i, 0)),
            pl.BlockSpec((1, N_HEADS, ROWS, HEAD_DIM),
                         lambda i: (i // BLK_PER_SEG, 0, i % BLK_PER_SEG, 0)),
            pl.BlockSpec((1, 1, D_MODEL), lambda i: (i // BLK_PER_BATCH, 0, 0)),
            pl.BlockSpec((SSM_WIDTH, SSM_WIDTH), lambda i: (0, 0)),
            pl.BlockSpec((D_MODEL, D_MODEL), lambda i: (0, 0)),
        ],
        out_specs=x_spec,
        out_shape=jax.ShapeDtypeStruct((TOKENS, D_MODEL), F32),
        scratch_shapes=[pltpu.VMEM((ROWS, D_MODEL), BF16)],
        compiler_params=pltpu.CompilerParams(
            dimension_semantics=("parallel",), vmem_limit_bytes=VMEM_LIMIT),
        name="mix",
    )(x2, y_il, att, gate1, w_glu_bf, w_out_bf)


def _ffn_kernel(x_ref, sh_ref, sc_ref, g_ref, nw_ref, wg_ref, wu_ref, wd_ref, o_ref, h_scr):
    j = pl.program_id(1)

    def hidden_tile(rows):
        h = h_scr[rows, :]
        g = jnp.dot(h, wg_ref[...], preferred_element_type=F32)
        u = jnp.dot(h, wu_ref[...], preferred_element_type=F32)
        a = (g * _sigmoid(g) * u).astype(BF16)
        return jnp.dot(a, wd_ref[...], preferred_element_type=F32)

    @pl.when(j == 0)
    def _():
        sub = FFN_TM // FFN_PROLOGUE_SPLIT
        for r in range(FFN_PROLOGUE_SPLIT):
            rows = slice(r * sub, (r + 1) * sub)
            h_scr[rows, :] = _modulated_norm(x_ref[rows, :], nw_ref[...], sh_ref[0], sc_ref[0])
        for r in range(FFN_PROLOGUE_SPLIT):
            rows = slice(r * sub, (r + 1) * sub)
            o_ref[rows, :] = hidden_tile(rows)

    @pl.when(j > 0)
    def _():
        o_ref[...] += hidden_tile(slice(None))

    @pl.when(j == FFN_HIDDEN // FFN_TH - 1)
    def _():
        o_ref[...] = x_ref[...] + g_ref[0] * o_ref[...]


def _ffn(x1, shift2, scale2, gate2, norm2_w, wg_bf, wu_bf, wd_bf):
    tm, th = FFN_TM, FFN_TH
    per_batch = SEQ // tm
    mod_spec = pl.BlockSpec((1, 1, D_MODEL), lambda i, j: (i // per_batch, 0, 0))
    return pl.pallas_call(
        _ffn_kernel,
        grid=(TOKENS // tm, FFN_HIDDEN // th),
        in_specs=[
            pl.BlockSpec((tm, D_MODEL), lambda i, j: (i, 0), pipeline_mode=pl.Buffered(1)),
            mod_spec, mod_spec, mod_spec,
            pl.BlockSpec((1, D_MODEL), lambda i, j: (0, 0)),
            pl.BlockSpec((D_MODEL, th), lambda i, j: (0, j)),
            pl.BlockSpec((D_MODEL, th), lambda i, j: (0, j)),
            pl.BlockSpec((th, D_MODEL), lambda i, j: (j, 0)),
        ],
        out_specs=pl.BlockSpec((tm, D_MODEL), lambda i, j: (i, 0)),
        out_shape=jax.ShapeDtypeStruct((TOKENS, D_MODEL), F32),
        scratch_shapes=[pltpu.VMEM((tm, D_MODEL), BF16)],
        compiler_params=pltpu.CompilerParams(
            dimension_semantics=("parallel", "arbitrary"), vmem_limit_bytes=VMEM_LIMIT),
        name="ffn",
    )(x1, shift2, scale2, gate2, norm2_w, wg_bf, wu_bf, wd_bf)


def kernel(x, c, w_ada, b_ada, norm1_w, w_in, ssm_a_re, ssm_a_im, ssm_log_dt, ssm_b_re, ssm_b_im,
           ssm_c_re, ssm_c_im, ssm_d, ssm_w_glu, q_norm_w, k_norm_w, w_out, norm2_w,
           w_ffn_gate, w_ffn_up, w_ffn_down):
    x2 = x.reshape(TOKENS, D_MODEL)
    for i in range(w_ada.shape[0]):
        mod = _ada(c, w_ada[i], b_ada[i])
        shift1, scale1, gate1, shift2, scale2, gate2 = [
            m.reshape(BATCH, 1, D_MODEL) for m in jnp.split(mod, N_MOD, axis=-1)]

        a_all, b_exp, c_exp = _ssm_prep(ssm_a_re[i], ssm_a_im[i], ssm_log_dt[i], ssm_b_re[i],
                                        ssm_b_im[i], ssm_c_re[i], ssm_c_im[i])
        y_il, q, k, v = _front(x2, shift1, scale1, norm1_w[i].reshape(1, D_MODEL),
                               w_in[i].astype(BF16),
                               q_norm_w[i].reshape(1, HEAD_DIM), k_norm_w[i].reshape(1, HEAD_DIM),
                               a_all, b_exp, c_exp, ssm_d[i].reshape(1, SSM_WIDTH))
        att = _attention(q, k, v)

        x2 = _mix(x2, y_il, att, gate1, ssm_w_glu[i].astype(BF16), w_out[i].astype(BF16))
        x2 = _ffn(x2, shift2, scale2, gate2,
                  norm2_w[i].reshape(1, D_MODEL), w_ffn_gate[i].astype(BF16),
                  w_ffn_up[i].astype(BF16), w_ffn_down[i].astype(BF16))
    return x2.reshape(BATCH, SEQ, D_MODEL)
```

```python
import numpy as np
import jax
import jax.numpy as jnp
from jax import lax
from jax.experimental import pallas as pl
from jax.experimental.pallas import tpu as pltpu

F32 = jnp.float32
BF16 = jnp.bfloat16

D_MODEL = 2048
BATCH = 2
SEQ = 8192
TOKENS = BATCH * SEQ
SSM_WIDTH = 1024
SSM_GROUP = 16
SSM_GROUPS = 64
SSM_STATE = 64
ATTN_WIDTH = 1024
HEAD_DIM = 128
N_HEADS = 8
BRANCH_DILATIONS = (16, 4, 1)
BRANCH_BLOCK = 128
FFN_HIDDEN = 5632
N_MOD = 6
NORM_EPS = 1e-6
MASK_VALUE = -1e30
LANES = 128

N_SEG = 8
SEG_LEN = TOKENS // N_SEG
SEG_PER_BATCH = SEQ // SEG_LEN
ROWS = 512
BLK_PER_SEG = SEG_LEN // ROWS
BLK_PER_BATCH = SEQ // ROWS

SUBLANES = 8
SC_TAU = 32
SC_LEN = SUBLANES * SC_TAU
SC_PER_ROWBLK = ROWS // SC_LEN
SSM_LB = SSM_WIDTH // LANES
SSM_COLSETS = 4
COLSET_CH = SSM_WIDTH // SSM_COLSETS
COLSET_ST = 16 * SSM_STATE

ATT_Q = SEG_LEN // 4

ADA_TN = 1024
FFN_TM = 1024
FFN_TH = 512
FFN_PROLOGUE_SPLIT = 4

VMEM_LIMIT = 56 * 1024 * 1024
FRONT_VMEM_LIMIT = 60 * 1024 * 1024


def _sigmoid(x):
    return 1.0 / (1.0 + jnp.exp(-x))


def _modulated_norm(x, nw, shift, scale):
    ms = jnp.mean(x * x, axis=-1, keepdims=True)
    y = x * lax.rsqrt(ms + NORM_EPS) * nw
    return (y * (1.0 + scale) + shift).astype(BF16)


def _interleaved_pieces():
    return [(slice(sc * SC_LEN + s * SC_TAU, sc * SC_LEN + (s + 1) * SC_TAU), sc * SC_LEN + s)
            for sc in range(SC_PER_ROWBLK) for s in range(SUBLANES)]


def _ada_kernel(ct_ref, w_ref, b_ref, o_ref, cond_scr):
    @pl.when(pl.program_id(0) == 0)
    def _():
        c = ct_ref[...]
        cond = c * _sigmoid(c)
        for b in range(BATCH):
            cond_scr[b] = jnp.broadcast_to(cond[:, b:b + 1], (D_MODEL, LANES))

    nlb = ADA_TN // LANES

    def body(kc, acc):
        row = pl.multiple_of(kc * 8, 8)
        out = []
        for b in range(BATCH):
            ck = cond_scr[b, pl.ds(row, 8), :]
            for lb in range(nlb):
                wk = w_ref[pl.ds(row, 8), lb * LANES:(lb + 1) * LANES]
                out.append(acc[b * nlb + lb] + ck * wk)
        return tuple(out)

    zero = jnp.zeros((8, LANES), F32)
    acc = lax.fori_loop(0, D_MODEL // 8, body, (zero,) * (BATCH * nlb), unroll=8)
    rows = [jnp.concatenate([jnp.sum(acc[b * nlb + lb], axis=0, keepdims=True)
                             for lb in range(nlb)], axis=1) for b in range(BATCH)]
    o_ref[...] = jnp.concatenate(rows, axis=0) + b_ref[...]


def _ada(c, w_ada, b_ada):
    n = N_MOD * D_MODEL
    return pl.pallas_call(
        _ada_kernel,
        grid=(n // ADA_TN,),
        in_specs=[
            pl.BlockSpec((D_MODEL, BATCH), lambda j: (0, 0)),
            pl.BlockSpec((D_MODEL, ADA_TN), lambda j: (0, j)),
            pl.BlockSpec((1, ADA_TN), lambda j: (0, j)),
        ],
        out_specs=pl.BlockSpec((BATCH, ADA_TN), lambda j: (0, j)),
        out_shape=jax.ShapeDtypeStruct((BATCH, n), F32),
        scratch_shapes=[pltpu.VMEM((BATCH, D_MODEL, LANES), F32)],
        compiler_params=pltpu.CompilerParams(
            dimension_semantics=("arbitrary",), vmem_limit_bytes=VMEM_LIMIT),
        name="ada",
    )(c.T, w_ada, b_ada.reshape(1, n))


def _head_rms(blk, w):
    ms = jnp.mean(blk * blk, axis=-1, keepdims=True)
    return blk * lax.rsqrt(ms + NORM_EPS) * w


def _front_kernel(x_ref, sh_ref, sc_ref, nw_ref, w_ref, qw_ref, kw_ref, b_ref, c_ref, a_ref, d_ref,
                  y_ref, q_ref, k_ref, v_ref, h_scr, u_scr, bu_scr, a_scr, carry_scr):
    @pl.when(pl.program_id(0) % BLK_PER_BATCH == 0)
    def _():
        carry_scr[...] = jnp.zeros_like(carry_scr)

    h_scr[...] = _modulated_norm(x_ref[...], nw_ref[...], sh_ref[0], sc_ref[0])
    h = h_scr[...]

    def proj(n):
        return jnp.dot(h, w_ref[:, n * 1024:(n + 1) * 1024], preferred_element_type=F32)

    def per_head(res, out_ref, fn):
        for hd in range(N_HEADS):
            out_ref[0, hd] = fn(res[:, hd * HEAD_DIM:(hd + 1) * HEAD_DIM]).astype(BF16)

    res = proj(0)
    for lb in range(SSM_LB):
        for rows, start in _interleaved_pieces():
            u_scr[lb, pl.ds(start, SC_TAU, stride=SUBLANES), :] = (
                res[rows, lb * LANES:(lb + 1) * LANES])

    heads = (
        lambda: per_head(proj(1), q_ref,
                         lambda blk: _head_rms(blk, qw_ref[...]) * (HEAD_DIM ** -0.5)),
        lambda: per_head(proj(2), k_ref, lambda blk: _head_rms(blk, kw_ref[...])),
        lambda: per_head(proj(3), v_ref, lambda blk: blk),
    )
    for s in range(SSM_COLSETS):
        def store_y(rows, y, s=s):
            y_ref[2 * s, rows, :] = y[:, :LANES]
            y_ref[2 * s + 1, rows, :] = y[:, LANES:]

        u = jnp.concatenate([u_scr[2 * s], u_scr[2 * s + 1]], axis=1)
        _ssm_block(u, b_ref.at[s], c_ref.at[s], a_ref[s],
                   d_ref[:, s * COLSET_CH:(s + 1) * COLSET_CH], store_y,
                   bu_scr.at[s % 2], a_scr.at[s], carry_scr.at[s])
        if s < len(heads):
            heads[s]()


def _front(x2, shift1, scale1, norm1_w, w_in_bf, q_norm_w, k_norm_w, a_all, b_exp, c_exp, d_flat):
    qkv_shape = jax.ShapeDtypeStruct((N_SEG, N_HEADS, SEG_LEN, HEAD_DIM), BF16)
    qkv_spec = pl.BlockSpec((1, N_HEADS, ROWS, HEAD_DIM),
                            lambda i: (i // BLK_PER_SEG, 0, i % BLK_PER_SEG, 0))
    mod_spec = pl.BlockSpec((1, 1, D_MODEL), lambda i: (i // BLK_PER_BATCH, 0, 0))

    def resident(shape):
        return pl.BlockSpec(shape, lambda i: (0,) * len(shape), pipeline_mode=pl.Buffered(1))

    states = pltpu.VMEM((SSM_COLSETS, SUBLANES, 2 * COLSET_ST), F32)
    return pl.pallas_call(
        _front_kernel,
        grid=(TOKENS // ROWS,),
        in_specs=[
            pl.BlockSpec((ROWS, D_MODEL), lambda i: (i, 0)),
            mod_spec, mod_spec,
            pl.BlockSpec((1, D_MODEL), lambda i: (0, 0)),
            resident((D_MODEL, 4 * 1024)),
            pl.BlockSpec((1, HEAD_DIM), lambda i: (0, 0)),
            pl.BlockSpec((1, HEAD_DIM), lambda i: (0, 0)),
            resident((SSM_COLSETS, COLSET_CH, 2 * COLSET_ST)),
            resident((SSM_COLSETS, 2 * COLSET_ST, COLSET_CH)),
            pl.BlockSpec((SSM_COLSETS, 1, 2 * COLSET_ST), lambda i: (0, 0, 0)),
            pl.BlockSpec((1, SSM_WIDTH), lambda i: (0, 0)),
        ],
        out_specs=[
            pl.BlockSpec((SSM_LB, ROWS, LANES), lambda i: (0, i, 0)),
            qkv_spec, qkv_spec, qkv_spec,
        ],
        out_shape=[
            jax.ShapeDtypeStruct((SSM_LB, TOKENS, LANES), F32),
            qkv_shape, qkv_shape, qkv_shape,
        ],
        scratch_shapes=[
            pltpu.VMEM((ROWS, D_MODEL), BF16),
            pltpu.VMEM((SSM_LB, ROWS, LANES), F32),
            pltpu.VMEM((2, ROWS, 2 * COLSET_ST), F32),
            states, states,
        ],
        compiler_params=pltpu.CompilerParams(
            dimension_semantics=("arbitrary",), vmem_limit_bytes=FRONT_VMEM_LIMIT),
        name="front",
    )(x2, shift1, scale1, norm1_w, w_in_bf, q_norm_w, k_norm_w, b_exp, c_exp, a_all, d_flat)


def _ssm_prep_kernel(are_ref, aim_ref, ldt_ref, bre_ref, bim_ref, cim_ref,
                     abr_ref, abi_ref, bbr_ref, bbi_ref, ncim_ref):
    a_re = are_ref[...]
    a_im = aim_ref[...]
    dt = jnp.exp(ldt_ref[...])
    mag = jnp.exp(a_re * dt)
    abar_re = mag * jnp.cos(a_im * dt)
    abar_im = mag * jnp.sin(a_im * dt)
    num_re = abar_re - 1.0
    num_im = abar_im
    den = a_re * a_re + a_im * a_im
    z_re = (num_re * a_re + num_im * a_im) / den
    z_im = (num_im * a_re - num_re * a_im) / den
    b_re = bre_ref[...]
    b_im = bim_ref[...]
    abr_ref[...] = abar_re
    abi_ref[...] = abar_im
    bbr_ref[...] = z_re * b_re - z_im * b_im
    bbi_ref[...] = z_re * b_im + z_im * b_re
    ncim_ref[...] = -cim_ref[...]


def _ssm_prep(a_re, a_im, log_dt, b_re, b_im, c_re, c_im):
    G, N, C = SSM_GROUPS, SSM_STATE, SSM_GROUP
    tile_n = lambda a: jnp.tile(a, (1, C))
    cn = lambda b: b.transpose(0, 2, 1).reshape(G, C * N)
    shp = jax.ShapeDtypeStruct((G, C * N), F32)
    abr, abi, bbr, bbi, ncim = pl.pallas_call(
        _ssm_prep_kernel,
        out_shape=[shp] * 5,
        name="ssm_prep",
    )(tile_n(a_re), tile_n(a_im), log_dt.reshape(G, 1), cn(b_re), cn(b_im),
      c_im.reshape(G, C * N))

    eye = jnp.eye(16, dtype=F32)
    a_all = jnp.stack([abr[:, :N], abi[:, :N]])
    a_all = a_all.reshape(2, SSM_COLSETS, 16 * N).transpose(1, 0, 2)
    a_all = a_all.reshape(SSM_COLSETS, 1, 2 * COLSET_ST)
    bb = jnp.stack([bbr, bbi]).reshape(2, SSM_COLSETS, 16, C, N)
    b_exp = jnp.einsum('psgcn,gh->sgcphn', bb, eye)
    b_exp = b_exp.reshape(SSM_COLSETS, COLSET_CH, 2 * COLSET_ST).astype(BF16)
    cc = jnp.stack([c_re.reshape(G, C * N), ncim]).reshape(2, SSM_COLSETS, 16, C, N)
    c_exp = jnp.einsum('psgcn,gh->sphngc', cc, eye)
    c_exp = c_exp.reshape(SSM_COLSETS, 2 * COLSET_ST, COLSET_CH).astype(BF16)
    return a_all, b_exp, c_exp


def _cmul(ar, ai, br, bi):
    return ar * br - ai * bi, ar * bi + ai * br


def _ssm_block(u, b_ref, c_ref, a_row, d_row, store_y, bu_scr, a_scr, carry_scr):
    S = COLSET_ST
    a_scr[...] = jnp.broadcast_to(a_row, (SUBLANES, 2 * S))

    def abar():
        return a_scr[:, :S], a_scr[:, S:]

    pw = [abar()]
    for _ in range(SC_TAU.bit_length() + 1):
        pw.append(_cmul(*pw[-1], *pw[-1]))
    p1, p2, p4 = pw[-3:]
    sub = lax.broadcasted_iota(jnp.int32, (SUBLANES, S), 0)

    def shifted(v, k):
        return jnp.where(sub >= k, pltpu.roll(v, k, 0), 0.0)

    def tile(base, t):
        return slice(base + t * SUBLANES, base + (t + 1) * SUBLANES)

    def scan(base, hr, hi, store):
        for t in range(1, SC_TAU):
            rows = tile(base, t)
            ar, ai = abar()
            hr, hi = (ar * hr - ai * hi + bu_scr[rows, :S],
                      ar * hi + ai * hr + bu_scr[rows, S:])
            if store:
                bu_scr[rows, :S] = hr
                bu_scr[rows, S:] = hi
        return hr, hi

    u_bf = u.astype(BF16)

    def project_in(k):
        rows = slice(k * SC_LEN, (k + 1) * SC_LEN)
        bu_scr[rows, :] = jnp.dot(u_bf[rows], b_ref[...], preferred_element_type=F32)

    def project_out(k):
        rows = slice(k * SC_LEN, (k + 1) * SC_LEN)
        y = jnp.dot(bu_scr[rows, :].astype(BF16), c_ref[...], preferred_element_type=F32)
        store_y(rows, jax.nn.gelu(y + d_row * u[rows]))

    cr, ci = carry_scr[:, :S], carry_scr[:, S:]
    project_in(0)
    for k in range(SC_PER_ROWBLK):
        if k + 1 < SC_PER_ROWBLK:
            project_in(k + 1)
        base = k * SC_LEN
        first = tile(base, 0)
        er, ei = scan(base, bu_scr[first, :S], bu_scr[first, S:], False)
        xr = jnp.where(sub == 0, cr, pltpu.roll(er, 1, 0))
        xi = jnp.where(sub == 0, ci, pltpu.roll(ei, 1, 0))
        for (pr, pi), sh in ((p1, 1), (p2, 2), (p4, 4)):
            tr, ti = _cmul(pr, pi, shifted(xr, sh), shifted(xi, sh))
            xr, xi = xr + tr, xi + ti
        tr, ti = _cmul(*p1, xr, xi)
        cr, ci = pltpu.roll(tr + er, 1, 0), pltpu.roll(ti + ei, 1, 0)
        tr, ti = _cmul(*abar(), xr, xi)
        hr, hi = bu_scr[first, :S] + tr, bu_scr[first, S:] + ti
        bu_scr[first, :S] = hr
        bu_scr[first, S:] = hi
        scan(base, hr, hi, True)
        project_out(k)
    carry_scr[:, :S] = cr
    carry_scr[:, S:] = ci


def _branch_distances():
    out = []
    for dil in BRANCH_DILATIONS:
        per = 32 if dil == 1 else BRANCH_BLOCK
        idx = np.arange(BRANCH_BLOCK)
        pos = (BRANCH_BLOCK // per) * (idx % per) + idx // per
        dist = np.concatenate([pos[:, None] - pos[None, :] + BRANCH_BLOCK,
                               pos[:, None] - pos[None, :]], axis=1)
        ok = (dist >= 0) & (dist <= BRANCH_BLOCK)
        out.append(np.where(ok, dil * dist, -1).astype(np.float32))
    return np.stack(out)


def _attn_tile(qt, kt, vt, bm):
    s = lax.dot_general(qt, kt, (((1,), (1,)), ((), ())), preferred_element_type=F32)
    s = jnp.where(bm > 0.5 * MASK_VALUE, s + bm, MASK_VALUE)
    m = jnp.max(s, axis=-1, keepdims=True)
    p = jnp.exp(s - m)
    l = jnp.sum(p, axis=-1, keepdims=True)
    o = jnp.dot(p.astype(BF16), vt, preferred_element_type=F32)
    return m, l, o


def _attn_kernel(q_ref, k_ref, v_ref, dd_ref, o_ref,
                 nat_scr, q4, k4, v4, m4, l4, o4, bias_scr):
    h = pl.program_id(1)
    i = pl.program_id(2)
    Q = ATT_Q
    E = HEAD_DIM
    BLK = BRANCH_BLOCK
    cur = pl.multiple_of((i % 2) * Q, Q)
    prv = pl.multiple_of(Q - (i % 2) * Q, Q)

    def to_slabs(src_ref, dst, base):
        nat_scr[...] = src_ref[0, 0].astype(F32)
        for r in range(4):
            dst[r, pl.ds(base, Q), :] = nat_scr[pl.ds(r, Q, stride=4), :]

    to_slabs(q_ref, q4, 0)
    to_slabs(k_ref, k4, cur)
    to_slabs(v_ref, v4, cur)

    @pl.when(i == 0)
    def _():
        for r in range(4):
            k4[r, pl.ds(prv, Q), :] = jnp.zeros((Q, E), F32)
            v4[r, pl.ds(prv, Q), :] = jnp.zeros((Q, E), F32)

    hv = jnp.full((BLK, 2 * BLK), h, jnp.int32).astype(F32)
    slope = jnp.exp2(-8.0 * (hv + 1.0) / N_HEADS)
    is_prev = lax.broadcasted_iota(jnp.int32, (BLK, 2 * BLK), 1) < BLK
    for br in range(3):
        dd = dd_ref[br]
        bm = jnp.where(dd >= 0.0, (-slope) * dd, MASK_VALUE)
        bias_scr[2 * br] = bm
        bias_scr[2 * br + 1] = jnp.where(is_prev, MASK_VALUE, bm)

    def load(scr, chunks):
        return jnp.concatenate([scr[c] for c in chunks], axis=0)

    def merge(chunks, nrows, m_t, l_t, o_t, first, last):
        for n, c in enumerate(chunks):
            sl = slice(n * nrows, (n + 1) * nrows)
            m_c = jnp.broadcast_to(m_t[sl], (nrows, E))
            l_c = jnp.broadcast_to(l_t[sl], (nrows, E))
            o_c = o_t[sl]
            if first:
                m4[c] = m_c
                l4[c] = l_c
                o4[c] = o_c
                continue
            m_old = m4[c]
            m_new = jnp.maximum(m_old, m_c)
            alpha = jnp.exp(m_old - m_new)
            beta = jnp.exp(m_c - m_new)
            l_new = alpha * l4[c] + beta * l_c
            o_new = alpha * o4[c] + beta * o_c
            if last:
                o4[c] = o_new / l_new
            else:
                m4[c] = m_new
                l4[c] = l_new
                o4[c] = o_new

    def run_tile(br, q_chunks, kp_chunks, kc_chunks, nrows, seq_start):
        qt = load(q4, q_chunks).astype(BF16)
        kt = jnp.concatenate([load(k4, kp_chunks), load(k4, kc_chunks)], axis=0).astype(BF16)
        vt = jnp.concatenate([load(v4, kp_chunks), load(v4, kc_chunks)], axis=0).astype(BF16)
        if seq_start:
            bm = bias_scr[2 * br + (i == 0).astype(jnp.int32)]
        else:
            bm = bias_scr[2 * br]
        m_t, l_t, o_t = _attn_tile(qt, kt, vt, bm)
        merge(q_chunks, nrows, m_t, l_t, o_t, br == 0, br == 2)

    for r in range(4):
        for r2 in range(4):
            run_tile(0,
                     [(r, pl.ds(r2, BLK, stride=4), slice(None))],
                     [(r, pl.ds(prv + r2, BLK, stride=4), slice(None))],
                     [(r, pl.ds(cur + r2, BLK, stride=4), slice(None))],
                     BLK, True)

    for r in range(4):
        for b in range(4):
            kp = prv + (Q - BLK) if b == 0 else cur + (b - 1) * BLK
            run_tile(1,
                     [(r, pl.ds(b * BLK, BLK), slice(None))],
                     [(r, pl.ds(pl.multiple_of(kp, BLK), BLK), slice(None))],
                     [(r, pl.ds(pl.multiple_of(cur + b * BLK, BLK), BLK), slice(None))],
                     BLK, b == 0)

    for b in range(SEG_LEN // BLK):
        kp = prv + (Q - 32) if b == 0 else cur + (b - 1) * 32
        run_tile(2,
                 [(r, pl.ds(b * 32, 32), slice(None)) for r in range(4)],
                 [(r, pl.ds(pl.multiple_of(kp, 32), 32), slice(None)) for r in range(4)],
                 [(r, pl.ds(pl.multiple_of(cur + b * 32, 32), 32), slice(None))
                  for r in range(4)],
                 32, b == 0)

    for r in range(4):
        nat_scr[pl.ds(r, Q, stride=4), :] = o4[r]
    o_ref[0, 0] = nat_scr[...].astype(BF16)


def _attention(q, k, v):
    blk = pl.BlockSpec((1, 1, SEG_LEN, HEAD_DIM),
                       lambda b, h, i: (b * SEG_PER_BATCH + i, h, 0, 0))
    dd = jnp.asarray(_branch_distances())
    slab = pltpu.VMEM((4, ATT_Q, HEAD_DIM), F32)
    slab2 = pltpu.VMEM((4, 2 * ATT_Q, HEAD_DIM), F32)
    return pl.pallas_call(
        _attn_kernel,
        grid=(BATCH, N_HEADS, SEG_PER_BATCH),
        in_specs=[blk, blk, blk,
                  pl.BlockSpec((3, BRANCH_BLOCK, 2 * BRANCH_BLOCK), lambda b, h, i: (0, 0, 0))],
        out_specs=blk,
        out_shape=jax.ShapeDtypeStruct((N_SEG, N_HEADS, SEG_LEN, HEAD_DIM), BF16),
        scratch_shapes=[
            pltpu.VMEM((SEG_LEN, HEAD_DIM), F32),
            slab, slab2, slab2,
            slab, slab, slab,
            pltpu.VMEM((6, BRANCH_BLOCK, 2 * BRANCH_BLOCK), F32),
        ],
        compiler_params=pltpu.CompilerParams(
            dimension_semantics=("parallel", "parallel", "arbitrary"),
            vmem_limit_bytes=VMEM_LIMIT),
        name="attn",
    )(q, k, v, dd)


def _mix_kernel(x_ref, y_ref, att_ref, g_ref, wglu_ref, wout_ref, o_ref, cat_scr):
    y = jnp.concatenate(
        [jnp.concatenate([y_ref[lb, pl.ds(start, SC_TAU, stride=SUBLANES), :]
                          for _, start in _interleaved_pieces()], axis=0)
         for lb in range(SSM_LB)], axis=1)
    gl = jnp.dot(y.astype(BF16), wglu_ref[...], preferred_element_type=F32)
    cat_scr[:, :SSM_WIDTH] = (y * _sigmoid(gl)).astype(BF16)
    for h in range(N_HEADS):
        c0 = SSM_WIDTH + h * HEAD_DIM
        cat_scr[:, c0:c0 + HEAD_DIM] = att_ref[0, h]
    mixed = jnp.dot(cat_scr[...], wout_ref[...], preferred_element_type=F32)
    o_ref[...] = x_ref[...] + g_ref[0] * mixed


def _mix(x2, y_il, att, gate1, w_glu_bf, w_out_bf):
    x_spec = pl.BlockSpec((ROWS, D_MODEL), lambda i: (i, 0))
    return pl.pallas_call(
        _mix_kernel,
        grid=(TOKENS // ROWS,),
        in_specs=[
            x_spec,
            pl.BlockSpec((SSM_LB, ROWS, LANES), lambda i: (0, i, 0)),
            pl.BlockSpec((1, N_HEADS, ROWS, HEAD_DIM),
                         lambda i: (i // BLK_PER_SEG, 0, i % BLK_PER_SEG, 0)),
            pl.BlockSpec((1, 1, D_MODEL), lambda i: (i // BLK_PER_BATCH, 0, 0)),
            pl.BlockSpec((SSM_WIDTH, SSM_WIDTH), lambda i: (0, 0)),
            pl.BlockSpec((D_MODEL, D_MODEL), lambda i: (0, 0)),
        ],
        out_specs=x_spec,
        out_shape=jax.ShapeDtypeStruct((TOKENS, D_MODEL), F32),
        scratch_shapes=[pltpu.VMEM((ROWS, D_MODEL), BF16)],
        compiler_params=pltpu.CompilerParams(
            dimension_semantics=("parallel",), vmem_limit_bytes=VMEM_LIMIT),
        name="mix",
    )(x2, y_il, att, gate1, w_glu_bf, w_out_bf)


def _ffn_kernel(x_ref, sh_ref, sc_ref, g_ref, nw_ref, wg_ref, wu_ref, wd_ref, o_ref, h_scr):
    j = pl.program_id(1)

    def hidden_tile(rows):
        h = h_scr[rows, :]
        g = jnp.dot(h, wg_ref[...], preferred_element_type=F32)
        u = jnp.dot(h, wu_ref[...], preferred_element_type=F32)
        a = (g * _sigmoid(g) * u).astype(BF16)
        return jnp.dot(a, wd_ref[...], preferred_element_type=F32)

    @pl.when(j == 0)
    def _():
        sub = FFN_TM // FFN_PROLOGUE_SPLIT
        for r in range(FFN_PROLOGUE_SPLIT):
            rows = slice(r * sub, (r + 1) * sub)
            h_scr[rows, :] = _modulated_norm(x_ref[rows, :], nw_ref[...], sh_ref[0], sc_ref[0])
        for r in range(FFN_PROLOGUE_SPLIT):
            rows = slice(r * sub, (r + 1) * sub)
            o_ref[rows, :] = hidden_tile(rows)

    @pl.when(j > 0)
    def _():
        o_ref[...] += hidden_tile(slice(None))

    @pl.when(j == FFN_HIDDEN // FFN_TH - 1)
    def _():
        o_ref[...] = x_ref[...] + g_ref[0] * o_ref[...]


def _ffn(x1, shift2, scale2, gate2, norm2_w, wg_bf, wu_bf, wd_bf):
    tm, th = FFN_TM, FFN_TH
    per_batch = SEQ // tm
    mod_spec = pl.BlockSpec((1, 1, D_MODEL), lambda i, j: (i // per_batch, 0, 0))
    return pl.pallas_call(
        _ffn_kernel,
        grid=(TOKENS // tm, FFN_HIDDEN // th),
        in_specs=[
            pl.BlockSpec((tm, D_MODEL), lambda i, j: (i, 0)),
            mod_spec, mod_spec, mod_spec,
            pl.BlockSpec((1, D_MODEL), lambda i, j: (0, 0)),
            pl.BlockSpec((D_MODEL, th), lambda i, j: (0, j)),
            pl.BlockSpec((D_MODEL, th), lambda i, j: (0, j)),
            pl.BlockSpec((th, D_MODEL), lambda i, j: (j, 0)),
        ],
        out_specs=pl.BlockSpec((tm, D_MODEL), lambda i, j: (i, 0)),
        out_shape=jax.ShapeDtypeStruct((TOKENS, D_MODEL), F32),
        scratch_shapes=[pltpu.VMEM((tm, D_MODEL), BF16)],
        compiler_params=pltpu.CompilerParams(
            dimension_semantics=("parallel", "arbitrary"), vmem_limit_bytes=VMEM_LIMIT),
        name="ffn",
    )(x1, shift2, scale2, gate2, norm2_w, wg_bf, wu_bf, wd_bf)


def kernel(x, c, w_ada, b_ada, norm1_w, w_in, ssm_a_re, ssm_a_im, ssm_log_dt, ssm_b_re, ssm_b_im,
           ssm_c_re, ssm_c_im, ssm_d, ssm_w_glu, q_norm_w, k_norm_w, w_out, norm2_w,
           w_ffn_gate, w_ffn_up, w_ffn_down):
    x2 = x.reshape(TOKENS, D_MODEL)
    for i in range(w_ada.shape[0]):
        mod = _ada(c, w_ada[i], b_ada[i])
        shift1, scale1, gate1, shift2, scale2, gate2 = [
            m.reshape(BATCH, 1, D_MODEL) for m in jnp.split(mod, N_MOD, axis=-1)]

        a_all, b_exp, c_exp = _ssm_prep(ssm_a_re[i], ssm_a_im[i], ssm_log_dt[i], ssm_b_re[i],
                                        ssm_b_im[i], ssm_c_re[i], ssm_c_im[i])
        y_il, q, k, v = _front(x2, shift1, scale1, norm1_w[i].reshape(1, D_MODEL),
                               w_in[i].astype(BF16),
                               q_norm_w[i].reshape(1, HEAD_DIM), k_norm_w[i].reshape(1, HEAD_DIM),
                               a_all, b_exp, c_exp, ssm_d[i].reshape(1, SSM_WIDTH))
        att = _attention(q, k, v)

        x2 = _mix(x2, y_il, att, gate1, ssm_w_glu[i].astype(BF16), w_out[i].astype(BF16))
        x2 = _ffn(x2, shift2, scale2, gate2,
                  norm2_w[i].reshape(1, D_MODEL), w_ffn_gate[i].astype(BF16),
                  w_ffn_up[i].astype(BF16), w_ffn_down[i].astype(BF16))
    return x2.reshape(BATCH, SEQ, D_MODEL)
```
